```python
import math
import jax, jax.numpy as jnp
from jax import lax
import numpy as np


D_MODEL = 2048
BATCH = 4
SEQ = 4096
DEPTH = 2

CTX_LEN = 256
GRID_W = 64
EPS = 1e-6
N_BRANCH = 3

D_A = D_MODEL // 2
CONV_A_W = 31

N_DIFF_HEADS = 8
DIFF_QK_DIM = 64
DIFF_V_DIM = 2 * DIFF_QK_DIM
D_B = N_DIFF_HEADS * DIFF_V_DIM
ROPE_BASE = 10000.0
Q_BLOCK = 128

N_GLA_HEADS = 4
D_C = D_MODEL // 2
DK_C = D_C // 2
GLA_HK = DK_C // N_GLA_HEADS
GLA_HV = D_C // N_GLA_HEADS
GATE_RANK = 16
GATE_TAU = 16.0
GLA_CHUNK = 64

D_FF = 11 * D_MODEL // 4
CONV_F_W = 3

IN_SPLITS = (2 * D_A, 2 * N_DIFF_HEADS * DIFF_QK_DIM, 2 * N_DIFF_HEADS * DIFF_QK_DIM, D_B,
             DK_C, DK_C, D_C, D_C, 2 * GATE_RANK, N_BRANCH * D_MODEL)
D_IN = sum(IN_SPLITS)

kernel_name = 'hybrid_diffusion_trunk'


def rms_norm(x, g):
    xf = x.astype(jnp.float32)
    y = xf * lax.rsqrt(jnp.mean(xf * xf, axis=-1, keepdims=True) + EPS)
    return (y * g.astype(jnp.float32)).astype(x.dtype)


def layer_norm(x, g, b):
    xf = x.astype(jnp.float32)
    mu = jnp.mean(xf, axis=-1, keepdims=True)
    xc = xf - mu
    var = jnp.mean(xc * xc, axis=-1, keepdims=True)
    return (xc * lax.rsqrt(var + EPS) * g.astype(jnp.float32) + b.astype(jnp.float32)).astype(x.dtype)


def modulate(h, shift, scale):
    return h * (1 + scale) + shift


def dw_conv(x, w, b):
    k = w.shape[0]
    pad = (k - 1) // 2
    y = lax.conv_general_dilated(x, w[:, None, :].astype(x.dtype), window_strides=(1,),
                                 padding=[(pad, pad)], dimension_numbers=('NWC', 'WIO', 'NWC'),
                                 feature_group_count=x.shape[-1])
    return y + b


def split_proj(u):
    return jnp.split(u, np.cumsum(IN_SPLITS)[:-1].tolist(), axis=-1)


def axial_rope(n_tokens):
    rows = n_tokens // GRID_W
    row = jnp.broadcast_to(jnp.arange(rows, dtype=jnp.float32)[:, None], (rows, GRID_W)).reshape(-1)
    col = jnp.broadcast_to(jnp.arange(GRID_W, dtype=jnp.float32)[None, :], (rows, GRID_W)).reshape(-1)
    n_freq = DIFF_QK_DIM // 4
    inv = ROPE_BASE ** (-jnp.arange(n_freq, dtype=jnp.float32) / n_freq)
    ang = jnp.concatenate([row[:, None] * inv, col[:, None] * inv], axis=-1)
    return jnp.cos(ang), jnp.sin(ang)


def apply_rope(x, cos, sin):
    half = x.shape[-1] // 2
    x1, x2 = x[..., :half], x[..., half:]
    cos = cos[:, None, :].astype(x.dtype)
    sin = sin[:, None, :].astype(x.dtype)
    return jnp.concatenate([x1 * cos - x2 * sin, x1 * sin + x2 * cos], axis=-1)


def conformer_branch(u, conv_w, conv_b, ln_g, ln_b, w_out):
    val, gate = jnp.split(u, 2, axis=-1)
    h = val * jax.nn.sigmoid(gate)
    h = dw_conv(h, conv_w, conv_b)
    h = jax.nn.silu(layer_norm(h, ln_g, ln_b))
    return h @ w_out


def diff_qkv(q, k, v, qn_g, kn_g, rope):
    b_, n_ = q.shape[:2]
    q = rms_norm(q.reshape(b_, n_, 2 * N_DIFF_HEADS, DIFF_QK_DIM), qn_g)
    k = rms_norm(k.reshape(b_, n_, 2 * N_DIFF_HEADS, DIFF_QK_DIM), kn_g)
    if rope is not None:
        q = apply_rope(q, rope[0], rope[1])
        k = apply_rope(k, rope[0], rope[1])
    q = q * DIFF_QK_DIM ** -0.5
    v = v.reshape(b_, n_, N_DIFF_HEADS, DIFF_V_DIM)
    return q.transpose(0, 2, 1, 3), k.transpose(0, 2, 1, 3), v.transpose(0, 2, 1, 3)


def diff_attend(q, k, v, lam, lam_init, subln_g):
    s = jnp.einsum('bhqd,bhkd->bhqk', q, k).astype(jnp.float32)
    p = jax.nn.softmax(s, axis=-1)
    b_, h2, nq, nk = p.shape
    p = p.reshape(b_, h2 // 2, 2, nq, nk)
    w = p[:, :, 0] - lam * p[:, :, 1]
    o = jnp.einsum('bhqk,bhkv->bhqv', w.astype(v.dtype), v)
    return rms_norm(o, subln_g) * (1.0 - lam_init)


def diff_attn_latent(q, k_all, v_all, lam, lam_init, subln_g):
    b_, h2, n_, d = q.shape
    nb = n_ // Q_BLOCK
    qb = q.reshape(b_, h2, nb, Q_BLOCK, d).transpose(2, 0, 1, 3, 4)
    ob = lax.map(lambda qq: diff_attend(qq, k_all, v_all, lam, lam_init, subln_g), qb)
    return ob.transpose(1, 0, 3, 2, 4).reshape(b_, n_, D_B)


def gla_heads(t, hd):
    b_, n_ = t.shape[:2]
    return t.reshape(b_, n_, N_GLA_HEADS, hd).transpose(0, 2, 1, 3)


def gla_log_gates(lr, w2, b):
    b_, n_ = lr.shape[:2]
    z = jnp.einsum('bnzr,zrk->bnzk', lr.reshape(b_, n_, 2, GATE_RANK), w2) + b
    la = jax.nn.log_sigmoid(z.astype(jnp.float32)) / GATE_TAU
    return gla_heads(la[:, :, 0], GLA_HK), gla_heads(la[:, :, 1], GLA_HK)


def gla_scan(q, k, v, log_a, s0):
    b_, h_, n_, dk = q.shape
    dv = v.shape[-1]
    nc = n_ // GLA_CHUNK
    mask = jnp.tril(jnp.ones((GLA_CHUNK, GLA_CHUNK), dtype=bool))

    def to_chunks(t):
        return t.reshape(b_, h_, nc, GLA_CHUNK, t.shape[-1]).transpose(2, 0, 1, 3, 4)

    def step(s, inp):
        qc, kc, vc, gc = (t.astype(jnp.float32) for t in inp)
        bcum = jnp.cumsum(gc, axis=2)
        diff = bcum[:, :, :, None, :] - bcum[:, :, None, :, :]
        decay = jnp.exp(jnp.where(mask[:, :, None], diff, -jnp.inf))
        a = jnp.einsum('bhid,bhjd,bhijd->bhij', qc, kc, decay)
        o = jnp.einsum('bhij,bhjv->bhiv', a, vc) + jnp.einsum('bhid,bhdv->bhiv', qc * jnp.exp(bcum), s)
        b_last = bcum[:, :, -1:, :]
        s_new = jnp.exp(b_last[:, :, 0, :])[..., None] * s + jnp.einsum('bhjd,bhjv->bhdv', kc * jnp.exp(b_last - bcum), vc)
        return s_new, o

    s_fin, o = lax.scan(step, s0, (to_chunks(q), to_chunks(k), to_chunks(v), to_chunks(log_a)))
    o = o.transpose(1, 2, 0, 3, 4).reshape(b_, h_, n_, dv)
    return o.astype(v.dtype), s_fin


def gla_bidir(q, k, v, la_f, la_b, s0_f, s0_b):
    o_f, s_f = gla_scan(q, k, v, la_f, s0_f)
    flip = lambda t: jnp.flip(t, axis=2)
    o_b, s_b = gla_scan(flip(q), flip(k), flip(v), flip(la_b), s0_b)
    return o_f + flip(o_b), s_f, s_b


def gla_inputs(qg, kg, vg, lr, w2, b):
    la_f, la_b = gla_log_gates(lr, w2, b)
    return gla_heads(qg * GLA_HK ** -0.5, GLA_HK), gla_heads(kg, GLA_HK), gla_heads(vg, GLA_HV), la_f, la_b


def gla_out(o, r, gn_g, w_out):
    b_, h_, n_, dv = o.shape
    o = rms_norm(o, gn_g).transpose(0, 2, 1, 3).reshape(b_, n_, h_ * dv)
    return (o * jax.nn.silu(r)) @ w_out


def merge_branches(ya, yb, yc, gate_pre, b_gate, w_o):
    b_, n_ = gate_pre.shape[:2]
    g = jax.nn.sigmoid(gate_pre.reshape(b_, n_, N_BRANCH, D_MODEL) + b_gate)
    return (g[:, :, 0] * ya + g[:, :, 1] * yb + g[:, :, 2] * yc) @ w_o


def conv_ffn(h, w_up, conv_w, conv_b, w_down):
    u = dw_conv(h @ w_up, conv_w, conv_b)
    val, gate = jnp.split(u, 2, axis=-1)
    return (jax.nn.silu(gate) * val) @ w_down


def setup_inputs(seed: int = 0) -> dict:
    key = jax.random.key(seed)
    ks = iter(jax.random.split(key, 40))
    L, D = DEPTH, D_MODEL

    def nrm(shape, scale):
        return jax.random.normal(next(ks), shape, jnp.float32) * scale

    return {
        'x': nrm((BATCH, SEQ, D), 1.0),
        'c': nrm((BATCH, D), 1.0),
        'ctx': nrm((BATCH, CTX_LEN, D), 1.0),
        'c_ctx': nrm((D,), 1.0),
        'w_ada': nrm((L, D, 6 * D), 0.5 * D ** -0.5),
        'b_ada': nrm((L, 6 * D), 0.02),
        'g_norm1': 1.0 + nrm((L, D), 0.02),
        'w_in': nrm((L, D, D_IN), D ** -0.5),
        'b_gate': nrm((L, N_BRANCH, D), 0.02),
        'conv_a_w': nrm((L, CONV_A_W, D_A), CONV_A_W ** -0.5),
        'conv_a_b': nrm((L, D_A), 0.02),
        'ln_a_g': 1.0 + nrm((L, D_A), 0.02),
        'ln_a_b': nrm((L, D_A), 0.02),
        'w_a_out': nrm((L, D_A, D), D_A ** -0.5),
        'qn_g': 1.0 + nrm((L, DIFF_QK_DIM), 0.02),
        'kn_g': 1.0 + nrm((L, DIFF_QK_DIM), 0.02),
        'lam_q1': nrm((L, DIFF_QK_DIM), 0.1),
        'lam_k1': nrm((L, DIFF_QK_DIM), 0.1),
        'lam_q2': nrm((L, DIFF_QK_DIM), 0.1),
        'lam_k2': nrm((L, DIFF_QK_DIM), 0.1),
        'subln_g': 1.0 + nrm((L, DIFF_V_DIM), 0.02),
        'w_b_out': nrm((L, D_B, D), D_B ** -0.5),
        'w_alpha2': nrm((L, 2, GATE_RANK, DK_C), GATE_RANK ** -0.5),
        'b_alpha': nrm((L, 2, DK_C), 0.02),
        'gn_c_g': 1.0 + nrm((L, GLA_HV), 0.02),
        'w_c_out': nrm((L, D_C, D), D_C ** -0.5),
        'w_o': nrm((L, D, D), D ** -0.5),
        'g_norm2': 1.0 + nrm((L, D), 0.02),
        'w_up': nrm((L, D, 2 * D_FF), D ** -0.5),
        'conv_f_w': nrm((L, CONV_F_W, 2 * D_FF), CONV_F_W ** -0.5),
        'conv_f_b': nrm((L, 2 * D_FF), 0.02),
        'w_down': nrm((L, D_FF, D), D_FF ** -0.5),
    }


def reference(x, c, ctx, c_ctx, w_ada, b_ada, g_norm1, w_in, b_gate, conv_a_w, conv_a_b, ln_a_g, ln_a_b,
              w_a_out, qn_g, kn_g, lam_q1, lam_k1, lam_q2, lam_k2, subln_g, w_b_out, w_alpha2, b_alpha,
              gn_c_g, w_c_out, w_o, g_norm2, w_up, conv_f_w, conv_f_b, w_down):
    b_ = x.shape[0]
    n_lat = x.shape[1]
    rope = axial_rope(n_lat)
    s_zero = jnp.zeros((b_, N_GLA_HEADS, GLA_HK, GLA_HV), jnp.float32)

    for l in range(DEPTH):
        last = l == DEPTH - 1
        lam_init = 0.8 - 0.6 * math.exp(-0.3 * l)

        mod_l = (jax.nn.silu(c) @ w_ada[l] + b_ada[l])[:, None, :]
        mod_c = jax.nn.silu(c_ctx) @ w_ada[l] + b_ada[l]
        sh1, sc1, gt1, sh2, sc2, gt2 = jnp.split(mod_l, 6, axis=-1)
        csh1, csc1, cgt1, csh2, csc2, cgt2 = jnp.split(mod_c, 6, axis=-1)

        h_l = modulate(rms_norm(x, g_norm1[l]), sh1, sc1)
        h_c = modulate(rms_norm(ctx, g_norm1[l]), csh1, csc1)
        a_l, qd_l, kd_l, vd_l, qg_l, kg_l, vg_l, r_l, lr_l, gate_l = split_proj(h_l @ w_in[l])
        a_c, qd_c, kd_c, vd_c, qg_c, kg_c, vg_c, r_c, lr_c, gate_c = split_proj(h_c @ w_in[l])

        lam = (jnp.exp(jnp.sum(lam_q1[l] * lam_k1[l]).astype(jnp.float32))
               - jnp.exp(jnp.sum(lam_q2[l] * lam_k2[l]).astype(jnp.float32)) + lam_init)
        q_l, k_l, v_l = diff_qkv(qd_l, kd_l, vd_l, qn_g[l], kn_g[l], rope)
        q_c, k_c, v_c = diff_qkv(qd_c, kd_c, vd_c, qn_g[l], kn_g[l], None)
        k_all = jnp.concatenate([k_l, k_c], axis=2)
        v_all = jnp.concatenate([v_l, v_c], axis=2)
        yb_l = diff_attn_latent(q_l, k_all, v_all, lam, lam_init, subln_g[l]) @ w_b_out[l]

        gq_c, gk_c, gv_c, laf_c, lab_c = gla_inputs(qg_c, kg_c, vg_c, lr_c, w_alpha2[l], b_alpha[l])
        gq_l, gk_l, gv_l, laf_l, lab_l = gla_inputs(qg_l, kg_l, vg_l, lr_l, w_alpha2[l], b_alpha[l])
        o_c, s_f, s_b = gla_bidir(gq_c, gk_c, gv_c, laf_c, lab_c, s_zero, s_zero)
        o_l, _, _ = gla_bidir(gq_l, gk_l, gv_l, laf_l, lab_l, s_f, s_b)
        yc_l = gla_out(o_l, r_l, gn_c_g[l], w_c_out[l])

        ya_l = conformer_branch(a_l, conv_a_w[l], conv_a_b[l], ln_a_g[l], ln_a_b[l], w_a_out[l])

        x_mid = x + gt1 * merge_branches(ya_l, yb_l, yc_l, gate_l, b_gate[l], w_o[l])

        h2 = modulate(rms_norm(x_mid, g_norm2[l]), sh2, sc2)
        x_new = x_mid + gt2 * conv_ffn(h2, w_up[l], conv_f_w[l], conv_f_b[l], w_down[l])

        if not last:
            yb_c = diff_attend(q_c, k_c, v_c, lam, lam_init, subln_g[l])
            yb_c = yb_c.transpose(0, 2, 1, 3).reshape(b_, ctx.shape[1], D_B) @ w_b_out[l]
            yc_c = gla_out(o_c, r_c, gn_c_g[l], w_c_out[l])
            ya_c = conformer_branch(a_c, conv_a_w[l], conv_a_b[l], ln_a_g[l], ln_a_b[l], w_a_out[l])
            ctx_mid = ctx + cgt1 * merge_branches(ya_c, yb_c, yc_c, gate_c, b_gate[l], w_o[l])
            h2c = modulate(rms_norm(ctx_mid, g_norm2[l]), csh2, csc2)
            ctx = ctx_mid + cgt2 * conv_ffn(h2c, w_up[l], conv_f_w[l], conv_f_b[l], w_down[l])

        x = x_new

    return x
```

```python
import functools
import math

import numpy as np
import jax
import jax.numpy as jnp
from jax import lax
from jax.experimental import pallas as pl
from jax.experimental.pallas import tpu as pltpu

F32 = jnp.float32
BF16 = jnp.bfloat16
HIGHEST = lax.Precision.HIGHEST

EPS = 1e-6
GRID_W = 64
ROPE_BASE = 10000.0
N_DIFF_HEADS = 8
DIFF_QK_DIM = 64
N_GLA_HEADS = 4
GATE_RANK = 16
GATE_TAU = 16.0
GLA_CHUNK = 64
CONV_A_W = 31

LANES = 128
VMEM_LIMIT_BYTES = 56 * 1024 * 1024

_NT = (((1,), (1,)), ((), ()))
_TN = (((0,), (0,)), ((), ()))


def _cparams(sem):
    return pltpu.CompilerParams(dimension_semantics=sem, vmem_limit_bytes=VMEM_LIMIT_BYTES)


def _sigmoid(x):
    return 1.0 / (1.0 + jnp.exp(-x))


def _silu(x):
    return x * _sigmoid(x)


def _adaln_kernel(c_ref, w_ref, b_ref, o_ref):
    a = _silu(c_ref[...])
    o_ref[...] = jnp.dot(a, w_ref[...], preferred_element_type=F32, precision=HIGHEST) + b_ref[...]


def _adaln(cvecs, w_ada, b_ada, layer):
    rows, d = cvecs.shape
    n = w_ada.shape[-1]
    tn = 1024
    return pl.pallas_call(
        _adaln_kernel,
        grid=(n // tn,),
        in_specs=[pl.BlockSpec((rows, d), lambda j: (0, 0)),
                  pl.BlockSpec((None, d, tn), lambda j: (layer, 0, j)),
                  pl.BlockSpec((None, 1, tn), lambda j: (layer, 0, j))],
        out_specs=pl.BlockSpec((rows, tn), lambda j: (0, j)),
        out_shape=jax.ShapeDtypeStruct((rows, n), F32),
        compiler_params=_cparams(("arbitrary",)),
        name="adaln",
    )(cvecs, w_ada, b_ada.reshape(b_ada.shape[0], 1, n))


def _mm_kernel(*refs, norm, residual, tm, rc):
    it = iter(refs)
    x_ref = next(it)
    if norm:
        g_ref, sh_ref, sc_ref = next(it), next(it), next(it)
    w_ref = next(it)
    if residual:
        res_ref, gate_ref = next(it), next(it)
    o_ref = next(it)
    if norm:
        h_ref = next(it)

        @pl.when(pl.program_id(2) == 0)
        def _():
            gain = g_ref[...]
            scale1 = 1.0 + sc_ref[0, 0]
            shift = sh_ref[0, 0]

            def body(r, carry):
                xs = x_ref[0, pl.ds(pl.multiple_of(r * rc, rc), rc), :]
                ms = jnp.mean(xs * xs, axis=-1, keepdims=True)
                y = xs * lax.rsqrt(ms + EPS) * gain
                h_ref[pl.ds(pl.multiple_of(r * rc, rc), rc), :] = (y * scale1 + shift).astype(BF16)
                return carry

            lax.fori_loop(0, tm // rc, body, 0)

        lhs = h_ref[...]
    else:
        lhs = x_ref[0]
    acc = jnp.dot(lhs, w_ref[...], preferred_element_type=F32)
    if residual:
        o_ref[0] = res_ref[0] + gate_ref[0, 0] * acc
    else:
        o_ref[0] = acc.astype(o_ref.dtype)


def _mm(x, w, *, tn, out_dtype, norm=None, residual=None, name):
    g_, r_, k_ = x.shape
    n_ = w.shape[1]
    tm = min(512, r_)
    assert r_ % tm == 0 and n_ % tn == 0
    grid = (g_, r_ // tm, n_ // tn)
    in_specs = [pl.BlockSpec((1, tm, k_), lambda g, i, j: (g, i, 0))]
    args = [x]
    scratch = []
    if norm is not None:
        gain, mod4, shi, sci, row0, rstr = norm
        in_specs += [pl.BlockSpec((1, k_), lambda g, i, j: (0, 0)),
                     pl.BlockSpec((1, 1, 1, k_), lambda g, i, j: (row0 + g * rstr, shi, 0, 0)),
                     pl.BlockSpec((1, 1, 1, k_), lambda g, i, j: (row0 + g * rstr, sci, 0, 0))]
        args += [gain, mod4, mod4]
        scratch = [pltpu.VMEM((tm, k_), BF16)]
    in_specs.append(pl.BlockSpec((k_, tn), lambda g, i, j: (0, j)))
    args.append(w)
    if residual is not None:
        res, mod4r, gti, rrow0, rrstr = residual
        in_specs += [pl.BlockSpec((1, tm, tn), lambda g, i, j: (g, i, j)),
                     pl.BlockSpec((1, 1, 1, tn), lambda g, i, j: (rrow0 + g * rrstr, gti, 0, j))]
        args += [res, mod4r]
    kern = functools.partial(_mm_kernel, norm=norm is not None, residual=residual is not None,
                             tm=tm, rc=min(128, tm))
    return pl.pallas_call(
        kern, grid=grid, in_specs=in_specs,
        out_specs=pl.BlockSpec((1, tm, tn), lambda g, i, j: (g, i, j)),
        out_shape=jax.ShapeDtypeStruct((g_, r_, n_), out_dtype),
        scratch_shapes=scratch,
        compiler_params=_cparams(("arbitrary", "arbitrary", "arbitrary")),
        name=name,
    )(*args)


def _qkprep_kernel(q_ref, k_ref, cos_ref, sin_ref, gq_ref, gk_ref, gm_ref, qo_ref, ko_ref, *, tm):
    cos = cos_ref[...]
    sin = sin_ref[...]
    gm = gm_ref[...]
    lane = lax.broadcasted_iota(jnp.int32, (tm, LANES), 1)
    first = (lane % DIFF_QK_DIM) < (DIFF_QK_DIM // 2)
    for src, g_ref, dst in ((q_ref, gq_ref, qo_ref), (k_ref, gk_ref, ko_ref)):
        gain = g_ref[...]
        for h in range(src.shape[-1] // LANES):
            sl = slice(h * LANES, (h + 1) * LANES)
            x = src[0, :, sl].astype(F32)
            sq = x * x
            hi = sq.astype(BF16)
            lo = (sq - hi.astype(F32)).astype(BF16)
            ss = (jnp.dot(hi, gm, preferred_element_type=F32) + jnp.dot(lo, gm, preferred_element_type=F32))
            y = x * lax.rsqrt(ss * (1.0 / DIFF_QK_DIM) + EPS) * gain
            partner = jnp.where(first, pltpu.roll(y, LANES - DIFF_QK_DIM // 2, 1),
                                pltpu.roll(y, DIFF_QK_DIM // 2, 1))
            dst[0, :, sl] = (y * cos + partner * sin).astype(BF16)


def _qkprep(u, cos_t, sin_t, gq, gk, gmat, qcb, kcb):
    b_, n_, _ = u.shape
    width = 2 * N_DIFF_HEADS * DIFF_QK_DIM
    tm = min(512, n_)
    kern = functools.partial(_qkprep_kernel, tm=tm)
    out = jax.ShapeDtypeStruct((b_, n_, width), BF16)
    return pl.pallas_call(
        kern, grid=(b_, n_ // tm),
        in_specs=[pl.BlockSpec((1, tm, width), lambda b, i: (b, i, qcb)),
                  pl.BlockSpec((1, tm, width), lambda b, i: (b, i, kcb)),
                  pl.BlockSpec((tm, LANES), lambda b, i: (i, 0)),
                  pl.BlockSpec((tm, LANES), lambda b, i: (i, 0)),
                  pl.BlockSpec((1, LANES), lambda b, i: (0, 0)),
                  pl.BlockSpec((1, LANES), lambda b, i: (0, 0)),
                  pl.BlockSpec((LANES, LANES), lambda b, i: (0, 0))],
        out_specs=[pl.BlockSpec((1, tm, width), lambda b, i: (b, i, 0)),
                   pl.BlockSpec((1, tm, width), lambda b, i: (b, i, 0))],
        out_shape=[out, out],
        compiler_params=_cparams(("arbitrary", "arbitrary")),
        name="qkprep",
    )(u, u, cos_t, sin_t, gq, gk, gmat)


def _attn_kernel(*refs, tq, has_ctx):
    if has_ctx:
        q_ref, kl_ref, vl_ref, kc_ref, vc_ref, lam_ref, og_ref, o_ref = refs
    else:
        q_ref, kl_ref, vl_ref, lam_ref, og_ref, o_ref = refs
    q = q_ref[0]
    lane = lax.broadcasted_iota(jnp.int32, (tq, LANES), 1)
    zero = jnp.zeros_like(q)
    qq = jnp.concatenate([jnp.where(lane < DIFF_QK_DIM, q, zero),
                          jnp.where(lane >= DIFF_QK_DIM, q, zero)], axis=0)
    sl = lax.dot_general(qq, kl_ref[0], _NT, preferred_element_type=F32)
    m = jnp.max(sl, axis=-1, keepdims=True)
    if has_ctx:
        sc = lax.dot_general(qq, kc_ref[0], _NT, preferred_element_type=F32)
        m = jnp.maximum(m, jnp.max(sc, axis=-1, keepdims=True))
    pl_ = jnp.exp(sl - m)
    den = jnp.sum(pl_, axis=-1, keepdims=True)
    if has_ctx:
        pc = jnp.exp(sc - m)
        den = den + jnp.sum(pc, axis=-1, keepdims=True)
    r = 1.0 / den
    lam = lam_ref[0:1, 0:1]
    r1 = r[:tq]
    r2 = r[tq:] * lam
    wl = (pl_[:tq] * r1 - pl_[tq:] * r2).astype(BF16)
    o = jnp.dot(wl, vl_ref[0], preferred_element_type=F32)
    if has_ctx:
        wc = (pc[:tq] * r1 - pc[tq:] * r2).astype(BF16)
        o = o + jnp.dot(wc, vc_ref[0], preferred_element_type=F32)
    o = o * lax.rsqrt(jnp.mean(o * o, axis=-1, keepdims=True) + EPS) * og_ref[...]
    o_ref[0] = o.astype(BF16)


def _attn(qp, kp, u, vcb, lam_v, og, ctx=None):
    b_, n_, _ = qp.shape
    tq = min(128, n_)
    has_ctx = ctx is not None
    in_specs = [pl.BlockSpec((1, tq, LANES), lambda b, h, i: (b, i, h)),
                pl.BlockSpec((1, n_, LANES), lambda b, h, i: (b, 0, h)),
                pl.BlockSpec((1, n_, LANES), lambda b, h, i: (b, 0, vcb + h))]
    args = [qp, kp, u]
    if has_ctx:
        kpc, uc = ctx
        nc = kpc.shape[1]
        in_specs += [pl.BlockSpec((1, nc, LANES), lambda b, h, i: (b, 0, h)),
                     pl.BlockSpec((1, nc, LANES), lambda b, h, i: (b, 0, vcb + h))]
        args += [kpc, uc]
    in_specs += [pl.BlockSpec((1, LANES), lambda b, h, i: (0, 0)),
                 pl.BlockSpec((1, LANES), lambda b, h, i: (0, 0))]
    args += [lam_v, og]
    kern = functools.partial(_attn_kernel, tq=tq, has_ctx=has_ctx)
    return pl.pallas_call(
        kern, grid=(b_, N_DIFF_HEADS, n_ // tq), in_specs=in_specs,
        out_specs=pl.BlockSpec((1, tq, LANES), lambda b, h, i: (b, i, h)),
        out_shape=jax.ShapeDtypeStruct((b_, n_, N_DIFF_HEADS * LANES), BF16),
        compiler_params=_cparams(("arbitrary", "arbitrary", "arbitrary")),
        name="diff_attn_ctx" if has_ctx else "diff_attn",
    )(*args)


def _gla_constants(tt, reverse):
    c = GLA_CHUNK
    idx = np.arange(tt)
    same_chunk = (idx[:, None] // c) == (idx[None, :] // c)
    if reverse:
        cum = same_chunk & (idx[None, :] >= idx[:, None])
    else:
        cum = same_chunk & (idx[None, :] <= idx[:, None])
    tot = same_chunk
    sels, masks = [], []
    s = c // 2
    while s >= 1:
        blk = idx // (2 * s)
        right = (idx % (2 * s)) >= s
        same = blk[:, None] == blk[None, :]
        if reverse:
            ref = blk * 2 * s + s
            mask = same & (~right)[:, None] & right[None, :]
        else:
            ref = blk * 2 * s + s - 1
            mask = same & right[:, None] & (~right)[None, :]
        sels.append(idx[None, :] == ref[:, None])
        masks.append(mask)
        s //= 2
    masks.append(idx[:, None] == idx[None, :])
    f = lambda a: jnp.asarray(np.asarray(a, np.float32))
    return f(cum), f(tot), f(np.stack(sels)), f(np.stack(masks))


def _gla_kernel(q_ref, k_ref, v_ref, lr_ref, w2_ref, b_ref, cum_ref, tot_ref, sel_ref, mask_ref, s0_ref,
                o_ref, sfin_ref, s_ref, *, tt, reverse, hk, hv):
    t = pl.program_id(1)

    @pl.when(t == 0)
    def _():
        s_ref[...] = s0_ref[0]

    n_lvl = sel_ref.shape[0]
    lr = lr_ref[0].astype(F32)
    z = jnp.dot(lr, w2_ref[...], preferred_element_type=F32, precision=HIGHEST) + b_ref[...]
    g = (jnp.minimum(z, 0.0) - jnp.log(1.0 + jnp.exp(-jnp.abs(z)))) * (1.0 / GATE_TAU)
    cum = jnp.dot(cum_ref[...], g, preferred_element_type=F32, precision=HIGHEST)
    tot = jnp.dot(tot_ref[...], g, preferred_element_type=F32, precision=HIGHEST)
    refs = [jnp.dot(sel_ref[l], cum, preferred_element_type=F32, precision=HIGHEST) for l in range(n_lvl)]
    q = q_ref[0].astype(F32) * (hk ** -0.5)
    k = k_ref[0].astype(F32)
    eq = jnp.exp(cum)
    ek = jnp.exp(tot - cum)
    nch = tt // GLA_CHUNK
    order = range(nch - 1, -1, -1) if reverse else range(nch)
    for h in range(N_GLA_HEADS):
        sl = slice(h * hk, (h + 1) * hk)
        qh, kh = q[:, sl], k[:, sl]
        vh = v_ref[0, :, h * hv:(h + 1) * hv]
        a = mask_ref[n_lvl] * lax.dot_general(qh.astype(BF16), kh.astype(BF16), _NT, preferred_element_type=F32)
        for l in range(n_lvl):
            d = cum[:, sl] - refs[l][:, sl]
            qt = (qh * jnp.exp(jnp.minimum(d, 0.0))).astype(BF16)
            kt = (kh * jnp.exp(jnp.minimum(-d, 0.0))).astype(BF16)
            a = a + mask_ref[l] * lax.dot_general(qt, kt, _NT, preferred_element_type=F32)
        o_h = jnp.dot(a.astype(BF16), vh, preferred_element_type=F32)
        qe = (qh * eq[:, sl]).astype(BF16)
        kd = (kh * ek[:, sl]).astype(BF16)
        outs = [None] * nch
        for c in order:
            rows = slice(c * GLA_CHUNK, (c + 1) * GLA_CHUNK)
            st = s_ref[h]
            outs[c] = o_h[rows] + lax.dot_general(qe[rows], st.astype(BF16), _NT, preferred_element_type=F32)
            dec = jnp.exp(tot[c * GLA_CHUNK:c * GLA_CHUNK + 1, sl])
            s_ref[h] = st * dec + lax.dot_general(vh[rows], kd[rows], _TN, preferred_element_type=F32)
        o_ref[0, :, h * hv:(h + 1) * hv] = jnp.concatenate(outs, axis=0) if nch > 1 else outs[0]

    @pl.when(t == pl.num_programs(1) - 1)
    def _():
        sfin_ref[0] = s_ref[...]


def _gla_dir(u, cbs, w2p, balpha, s0, reverse):
    b_, n_, _ = u.shape
    qcb, kcb, vcb, lrcb, dk, dv = cbs
    hk, hv = dk // N_GLA_HEADS, dv // N_GLA_HEADS
    tt = min(128, n_)
    nt = n_ // tt
    cum, tot, sel, mask = _gla_constants(tt, reverse)
    tmap = (lambda t: nt - 1 - t) if reverse else (lambda t: t)
    c2 = lambda b, t: (0, 0)
    c3 = lambda b, t: (0, 0, 0)
    kern = functools.partial(_gla_kernel, tt=tt, reverse=reverse, hk=hk, hv=hv)
    return pl.pallas_call(
        kern, grid=(b_, nt),
        in_specs=[pl.BlockSpec((1, tt, dk), lambda b, t: (b, tmap(t), qcb)),
                  pl.BlockSpec((1, tt, dk), lambda b, t: (b, tmap(t), kcb)),
                  pl.BlockSpec((1, tt, dv), lambda b, t: (b, tmap(t), vcb)),
                  pl.BlockSpec((1, tt, LANES), lambda b, t: (b, tmap(t), lrcb)),
                  pl.BlockSpec((LANES, dk), c2),
                  pl.BlockSpec((1, dk), c2),
                  pl.BlockSpec((tt, tt), c2),
                  pl.BlockSpec((tt, tt), c2),
                  pl.BlockSpec(sel.shape, c3),
                  pl.BlockSpec(mask.shape, c3),
                  pl.BlockSpec((1, N_GLA_HEADS, hv, hk), lambda b, t: (b, 0, 0, 0))],
        out_specs=[pl.BlockSpec((1, tt, dv), lambda b, t: (b, tmap(t), 0)),
                   pl.BlockSpec((1, N_GLA_HEADS, hv, hk), lambda b, t: (b, 0, 0, 0))],
        out_shape=[jax.ShapeDtypeStruct((b_, n_, dv), F32),
                   jax.ShapeDtypeStruct((b_, N_GLA_HEADS, hv, hk), F32)],
        scratch_shapes=[pltpu.VMEM((N_GLA_HEADS, hv, hk), F32)],
        compiler_params=_cparams(("arbitrary", "arbitrary")),
        name="gla_bwd" if reverse else "gla_fwd",
    )(u, u, u, u, w2p, balpha, cum, tot, sel, mask, s0)


def _gla_out_kernel(of_ref, ob_ref, r_ref, gn_ref, o_ref, *, hv):
    o = of_ref[0] + ob_ref[0]
    r = r_ref[0].astype(F32)
    gate = _silu(r)
    gn = gn_ref[...]
    for h in range(o.shape[-1] // hv):
        sl = slice(h * hv, (h + 1) * hv)
        oh = o[:, sl]
        y = oh * lax.rsqrt(jnp.mean(oh * oh, axis=-1, keepdims=True) + EPS) * gn
        o_ref[0, :, sl] = (y * gate[:, sl]).astype(BF16)


def _gla_out(o_f, o_b, u, rcb, gn):
    b_, n_, dv = o_f.shape
    tm = min(512, n_)
    kern = functools.partial(_gla_out_kernel, hv=dv // N_GLA_HEADS)
    blk = lambda cb: pl.BlockSpec((1, tm, dv), lambda b, i: (b, i, cb))
    return pl.pallas_call(
        kern, grid=(b_, n_ // tm),
        in_specs=[blk(0), blk(0), blk(rcb), pl.BlockSpec((1, dv // N_GLA_HEADS), lambda b, i: (0, 0))],
        out_specs=blk(0),
        out_shape=jax.ShapeDtypeStruct((b_, n_, dv), BF16),
        compiler_params=_cparams(("arbitrary", "arbitrary")),
        name="gla_out",
    )(o_f, o_b, u, gn)


HALO = 16


def _conformer_kernel(a_ref, p_ref, n_ref, cw_ref, cb_ref, lg_ref, lb_ref, o_ref, h_ref, *, tm, da):
    i = pl.program_id(1)
    last = pl.num_programs(1) - 1

    def glu(a):
        a = a.astype(F32)
        return a[:, :da] * _sigmoid(a[:, da:])

    h_ref[pl.ds(0, HALO), :] = jnp.where(i > 0, glu(p_ref[0]), 0.0)
    h_ref[pl.ds(HALO, tm), :] = glu(a_ref[0])
    h_ref[pl.ds(HALO + tm, HALO), :] = jnp.where(i < last, glu(n_ref[0]), 0.0)
    pad = (CONV_A_W - 1) // 2
    acc = jnp.zeros((tm, da), F32) + cb_ref[...]
    for k in range(CONV_A_W):
        acc = acc + cw_ref[k:k + 1, :] * h_ref[pl.ds(HALO - pad + k, tm), :]
    mu = jnp.mean(acc, axis=-1, keepdims=True)
    xc = acc - mu
    var = jnp.mean(xc * xc, axis=-1, keepdims=True)
    y = xc * lax.rsqrt(var + EPS) * lg_ref[...] + lb_ref[...]
    o_ref[0] = _silu(y).astype(BF16)


def _conformer(u, acb, conv_w, conv_b, ln_g, ln_b):
    b_, n_, _ = u.shape
    da = conv_w.shape[-1]
    tm = min(256, n_)
    nh = tm // HALO
    nblk = n_ // HALO
    kern = functools.partial(_conformer_kernel, tm=tm, da=da)
    row = lambda b, i: (0, 0)
    return pl.pallas_call(
        kern, grid=(b_, n_ // tm),
        in_specs=[pl.BlockSpec((1, tm, 2 * da), lambda b, i: (b, i, acb)),
                  pl.BlockSpec((1, HALO, 2 * da), lambda b, i: (b, jnp.maximum(i * nh - 1, 0), acb)),
                  pl.BlockSpec((1, HALO, 2 * da), lambda b, i: (b, jnp.minimum((i + 1) * nh, nblk - 1), acb)),
                  pl.BlockSpec((CONV_A_W, da), row),
                  pl.BlockSpec((1, da), row), pl.BlockSpec((1, da), row), pl.BlockSpec((1, da), row)],
        out_specs=pl.BlockSpec((1, tm, da), lambda b, i: (b, i, 0)),
        out_shape=jax.ShapeDtypeStruct((b_, n_, da), BF16),
        scratch_shapes=[pltpu.VMEM((tm + 2 * HALO, da), F32)],
        compiler_params=_cparams(("arbitrary", "arbitrary")),
        name="conformer",
    )(u, u, u, conv_w, conv_b, ln_g, ln_b)


def _merge_kernel(ha_ref, hb_ref, hc_ref, ga_ref, gb_ref, gc_ref, wa_ref, wb_ref, wc_ref, bg_ref, o_ref):
    acc = None
    for idx, (h_ref, g_ref, w_ref) in enumerate(((ha_ref, ga_ref, wa_ref), (hb_ref, gb_ref, wb_ref),
                                                 (hc_ref, gc_ref, wc_ref))):
        y = jnp.dot(h_ref[0], w_ref[...], preferred_element_type=F32)
        gate = _sigmoid(g_ref[0].astype(F32) + bg_ref[idx])
        acc = gate * y if acc is None else acc + gate * y
    o_ref[0] = acc.astype(BF16)


def _merge(ha, hb, hc, u, wa, wb, wc, bgate):
    b_, n_, kd = ha.shape
    d = wa.shape[1]
    tm = min(512, n_)
    tn = 512
    nj = d // tn
    hspec = pl.BlockSpec((1, tm, kd), lambda b, i, j: (b, i, 0))
    gspec = lambda br: pl.BlockSpec((1, tm, tn), lambda b, i, j: (b, i, br * nj + j))
    wspec = pl.BlockSpec((kd, tn), lambda b, i, j: (0, j))
    return pl.pallas_call(
        _merge_kernel, grid=(b_, n_ // tm, nj),
        in_specs=[hspec, hspec, hspec, gspec(0), gspec(1), gspec(2), wspec, wspec, wspec,
                  pl.BlockSpec((3, 1, tn), lambda b, i, j: (0, 0, j))],
        out_specs=pl.BlockSpec((1, tm, tn), lambda b, i, j: (b, i, j)),
        out_shape=jax.ShapeDtypeStruct((b_, n_, d), BF16),
        compiler_params=_cparams(("arbitrary", "arbitrary", "arbitrary")),
        name="merge",
    )(ha, hb, hc, u, u, u, wa, wb, wc, bgate)


def _ffn_down_kernel(uv_ref, ug_ref, pv_ref, nv_ref, pg_ref, ng_ref, cwv_ref, cwg_ref, cbv_ref, cbg_ref,
                     wd_ref, res_ref, gate_ref, o_ref, acc_ref, *, tm, tf):
    i = pl.program_id(1)
    j = pl.program_id(2)
    last_i = pl.num_programs(1) - 1
    row = lax.broadcasted_iota(jnp.int32, (tm, tf), 0)

    def conv(u_ref, p_ref, n_ref, cw_ref, cb_ref):
        u = u_ref[0].astype(F32)
        prev = jnp.where(i > 0, p_ref[0].astype(F32)[HALO - 1:HALO, :], 0.0)
        nxt = jnp.where(i < last_i, n_ref[0].astype(F32)[0:1, :], 0.0)
        up = jnp.where(row == 0, prev, pltpu.roll(u, 1, 0))
        dn = jnp.where(row == tm - 1, nxt, pltpu.roll(u, tm - 1, 0))
        return cw_ref[0:1, :] * up + cw_ref[1:2, :] * u + cw_ref[2:3, :] * dn + cb_ref[...]

    val = conv(uv_ref, pv_ref, nv_ref, cwv_ref, cbv_ref)
    gt = conv(ug_ref, pg_ref, ng_ref, cwg_ref, cbg_ref)
    act = (_silu(gt) * val).astype(BF16)
    part = jnp.dot(act, wd_ref[...], preferred_element_type=F32)

    @pl.when(j == 0)
    def _():
        acc_ref[...] = part

    @pl.when(j > 0)
    def _():
        acc_ref[...] += part

    @pl.when(j == pl.num_programs(2) - 1)
    def _():
        o_ref[0] = res_ref[0] + gate_ref[0, 0] * acc_ref[...]


def _ffn_down(uf, conv_w, conv_b, wd, res, mod4, gti, row0, rstr):
    b_, n_, f2 = uf.shape
    dff = f2 // 2
    d = wd.shape[1]
    tm = min(512, n_)
    tf = 512
    nf = dff // tf
    nh = tm // HALO
    nblk = n_ // HALO
    kern = functools.partial(_ffn_down_kernel, tm=tm, tf=tf)
    main = lambda off: pl.BlockSpec((1, tm, tf), lambda b, i, j: (b, i, off + j))
    prev = lambda off: pl.BlockSpec((1, HALO, tf), lambda b, i, j: (b, jnp.maximum(i * nh - 1, 0), off + j))
    nxt = lambda off: pl.BlockSpec((1, HALO, tf), lambda b, i, j: (b, jnp.minimum((i + 1) * nh, nblk - 1), off + j))
    cw = lambda off: pl.BlockSpec((3, tf), lambda b, i, j: (0, off + j))
    cb = lambda off: pl.BlockSpec((1, tf), lambda b, i, j: (0, off + j))
    return pl.pallas_call(
        kern, grid=(b_, n_ // tm, nf),
        in_specs=[main(0), main(nf), prev(0), nxt(0), prev(nf), nxt(nf), cw(0), cw(nf), cb(0), cb(nf),
                  pl.BlockSpec((tf, d), lambda b, i, j: (j, 0)),
                  pl.BlockSpec((1, tm, d), lambda b, i, j: (b, i, 0)),
                  pl.BlockSpec((1, 1, 1, d), lambda b, i, j: (row0 + b * rstr, gti, 0, 0))],
        out_specs=pl.BlockSpec((1, tm, d), lambda b, i, j: (b, i, 0)),
        out_shape=jax.ShapeDtypeStruct((b_, n_, d), F32),
        scratch_shapes=[pltpu.VMEM((tm, d), F32)],
        compiler_params=_cparams(("arbitrary", "arbitrary", "arbitrary")),
        name="ffn_down",
    )(uf, uf, uf, uf, uf, uf, conv_w, conv_w, conv_b, conv_b, wd, res, mod4)


def _rope_tables(n_tokens):
    rows = n_tokens // GRID_W
    row = jnp.broadcast_to(jnp.arange(rows, dtype=F32)[:, None], (rows, GRID_W)).reshape(-1)
    col = jnp.broadcast_to(jnp.arange(GRID_W, dtype=F32)[None, :], (rows, GRID_W)).reshape(-1)
    n_freq = DIFF_QK_DIM // 4
    inv = ROPE_BASE ** (-jnp.arange(n_freq, dtype=F32) / n_freq)
    ang = jnp.concatenate([row[:, None] * inv, col[:, None] * inv], axis=-1)
    cos, sin = jnp.cos(ang), jnp.sin(ang)
    cos_t = jnp.tile(cos, (1, LANES // cos.shape[1]))
    sin_t = jnp.tile(jnp.concatenate([-sin, sin], axis=-1), (1, LANES // (2 * sin.shape[1])))
    return cos_t, sin_t


def kernel(x, c, ctx, c_ctx, w_ada, b_ada, g_norm1, w_in, b_gate, conv_a_w, conv_a_b, ln_a_g, ln_a_b, w_a_out,
           qn_g, kn_g, lam_q1, lam_k1, lam_q2, lam_k2, subln_g, w_b_out, w_alpha2, b_alpha, gn_c_g, w_c_out,
           w_o, g_norm2, w_up, conv_f_w, conv_f_b, w_down):
    b_, n_lat, d = x.shape
    n_ctx = ctx.shape[1]
    depth = w_ada.shape[0]
    da = conv_a_w.shape[-1]
    qk_w = 2 * N_DIFF_HEADS * DIFF_QK_DIM
    db = w_b_out.shape[1]
    dc = w_c_out.shape[1]
    dk = w_alpha2.shape[-1]
    dff = w_down.shape[1]

    sizes = (2 * da, qk_w, qk_w, db, dk, dk, dc, dc, 2 * GATE_RANK, 3 * d)
    offs = np.concatenate([[0], np.cumsum(sizes)])
    o_a, o_qd, o_kd, o_vd, o_qg, o_kg, o_vg, o_r, o_lr, o_gate = (int(v) for v in offs[:-1])
    new_order = [(o_gate, 3 * d), (o_a, 2 * da), (o_qd, qk_w), (o_kd, qk_w), (o_vd, db), (o_vg, dc), (o_r, dc),
                 (o_qg, dk), (o_kg, dk), (o_lr, 2 * GATE_RANK)]
    starts = np.concatenate([[0], np.cumsum([s for _, s in new_order])])
    p_gate, p_a, p_qd, p_kd, p_vd, p_vg, p_r, p_qg, p_kg, p_lr = (int(v) for v in starts[:-1])
    tn_in = 768
    np_cols = -(-(p_lr + LANES) // tn_in) * tn_in
    assert p_a % (2 * da) == 0 and p_qd % qk_w == 0 and p_kd % qk_w == 0 and p_vd % LANES == 0
    assert p_vg % dc == 0 and p_r % dc == 0 and p_qg % dk == 0 and p_kg % dk == 0 and p_lr % LANES == 0

    def relayout_w_in(w):
        parts = [w[:, o:o + s] for o, s in new_order]
        parts.append(jnp.zeros((w.shape[0], np_cols - (p_lr + 2 * GATE_RANK)), w.dtype))
        return jnp.concatenate(parts, axis=1).astype(BF16)

    cos_l, sin_l = _rope_tables(n_lat)
    cos_c = jnp.ones((n_ctx, LANES), F32)
    sin_c = jnp.zeros((n_ctx, LANES), F32)
    grp = np.arange(LANES) // DIFF_QK_DIM
    gmat = jnp.asarray((grp[:, None] == grp[None, :]).astype(np.float32)).astype(BF16)

    cvecs = jnp.zeros((8, d), F32).at[:b_].set(c).at[b_].set(c_ctx)
    ctx_row = b_

    s_zero = jnp.zeros((b_, N_GLA_HEADS, dc // N_GLA_HEADS, dk // N_GLA_HEADS), F32)

    for l in range(depth):
        last = l == depth - 1
        lam_init = 0.8 - 0.6 * math.exp(-0.3 * l)
        lam = (jnp.exp(jnp.sum(lam_q1[l] * lam_k1[l])) - jnp.exp(jnp.sum(lam_q2[l] * lam_k2[l])) + lam_init)
        lam_v = jnp.full((1, LANES), lam, F32)
        og = (subln_g[l] * (1.0 - lam_init)).reshape(1, LANES)
        gq = (jnp.tile(qn_g[l], LANES // DIFF_QK_DIM) * DIFF_QK_DIM ** -0.5).reshape(1, LANES)
        gk = jnp.tile(kn_g[l], LANES // DIFF_QK_DIM).reshape(1, LANES)

        mod = _adaln(cvecs, w_ada, b_ada, l)
        mod4 = mod.reshape(8, 6, 1, d)

        w_in_p = relayout_w_in(w_in[l])
        wa, wb, wc = w_a_out[l].astype(BF16), w_b_out[l].astype(BF16), w_c_out[l].astype(BF16)
        wo = w_o[l].astype(BF16)
        wup = w_up[l].astype(BF16)
        wdn = w_down[l].astype(BF16)
        g1 = g_norm1[l].reshape(1, d)
        g2 = g_norm2[l].reshape(1, d)
        bg = b_gate[l].reshape(3, 1, d)
        w2 = w_alpha2[l]
        w2p = [jnp.zeros((LANES, dk), F32).at[z * GATE_RANK:(z + 1) * GATE_RANK].set(w2[z]) for z in range(2)]
        bal = [b_alpha[l, z].reshape(1, dk) for z in range(2)]
        gla_cbs = (p_qg // dk, p_kg // dk, p_vg // dc, p_lr // LANES, dk, dc)

        u_l = _mm(x, w_in_p, tn=tn_in, out_dtype=BF16, norm=(g1, mod4, 0, 1, 0, 1), name="in_proj")
        ctx_flat = ctx.reshape(1, b_ * n_ctx, d)
        u_c = _mm(ctx_flat, w_in_p, tn=tn_in, out_dtype=BF16, norm=(g1, mod4, 0, 1, ctx_row, 0),
                  name="in_proj_ctx").reshape(b_, n_ctx, np_cols)

        qp_l, kp_l = _qkprep(u_l, cos_l, sin_l, gq, gk, gmat, p_qd // qk_w, p_kd // qk_w)
        qp_c, kp_c = _qkprep(u_c, cos_c, sin_c, gq, gk, gmat, p_qd // qk_w, p_kd // qk_w)
        hb_l = _attn(qp_l, kp_l, u_l, p_vd // LANES, lam_v, og, ctx=(kp_c, u_c))

        of_c, sf = _gla_dir(u_c, gla_cbs, w2p[0], bal[0], s_zero, False)
        ob_c, sb = _gla_dir(u_c, gla_cbs, w2p[1], bal[1], s_zero, True)
        of_l, _ = _gla_dir(u_l, gla_cbs, w2p[0], bal[0], sf, False)
        ob_l, _ = _gla_dir(u_l, gla_cbs, w2p[1], bal[1], sb, True)
        gn = gn_c_g[l].reshape(1, -1)
        hc_l = _gla_out(of_l, ob_l, u_l, p_r // dc, gn)

        cb_a, lg_a, lb_a = conv_a_b[l].reshape(1, da), ln_a_g[l].reshape(1, da), ln_a_b[l].reshape(1, da)
        ha_l = _conformer(u_l, p_a // (2 * da), conv_a_w[l], cb_a, lg_a, lb_a)

        m_l = _merge(ha_l, hb_l, hc_l, u_l, wa, wb, wc, bg)
        x_mid = _mm(m_l, wo, tn=1024, out_dtype=F32, residual=(x, mod4, 2, 0, 1), name="out_proj")

        cfb = conv_f_b[l].reshape(1, 2 * dff)
        uf_l = _mm(x_mid, wup, tn=1024, out_dtype=BF16, norm=(g2, mod4, 3, 4, 0, 1), name="ffn_up")
        x_new = _ffn_down(uf_l, conv_f_w[l], cfb, wdn, x_mid, mod4, 5, 0, 1)

        if not last:
            hb_c = _attn(qp_c, kp_c, u_c, p_vd // LANES, lam_v, og)
            hc_c = _gla_out(of_c, ob_c, u_c, p_r // dc, gn)
            ha_c = _conformer(u_c, p_a // (2 * da), conv_a_w[l], cb_a, lg_a, lb_a)
            m_c = _merge(ha_c, hb_c, hc_c, u_c, wa, wb, wc, bg)
            ctx_mid = _mm(m_c.reshape(1, b_ * n_ctx, d), wo, tn=1024, out_dtype=F32,
                          residual=(ctx_flat, mod4, 2, ctx_row, 0), name="out_proj_ctx")
            uf_c = _mm(ctx_mid, wup, tn=1024, out_dtype=BF16, norm=(g2, mod4, 3, 4, ctx_row, 0),
                       name="ffn_up_ctx").reshape(b_, n_ctx, 2 * dff)
            ctx = _ffn_down(uf_c, conv_f_w[l], cfb, wdn, ctx_mid.reshape(b_, n_ctx, d), mod4, 5, ctx_row, 0)

        x = x_new

    return x
```

```python
import functools
import math

import numpy as np
import jax
import jax.numpy as jnp
from jax import lax
from jax.experimental import pallas as pl
from jax.experimental.pallas import tpu as pltpu

F32 = jnp.float32
BF16 = jnp.bfloat16
HIGHEST = lax.Precision.HIGHEST

EPS = 1e-6
GRID_W = 64
ROPE_BASE = 10000.0
N_DIFF_HEADS = 8
DIFF_QK_DIM = 64
N_GLA_HEADS = 4
GATE_RANK = 16
GATE_TAU = 16.0
GLA_CHUNK = 64
CONV_A_W = 31

LANES = 128
VMEM_LIMIT_BYTES = 56 * 1024 * 1024

_NT = (((1,), (1,)), ((), ()))
_TN = (((0,), (0,)), ((), ()))


def _cparams(sem):
    return pltpu.CompilerParams(dimension_semantics=sem, vmem_limit_bytes=VMEM_LIMIT_BYTES)


def _pick_tile(total, target):
    best = LANES
    for t in range(LANES, min(total, target) + 1, LANES):
        if total % t == 0:
            best = t
    return best


def _sigmoid(x):
    return 1.0 / (1.0 + jnp.exp(-x))


def _silu(x):
    return x * _sigmoid(x)


def _adaln_kernel(c_ref, w_ref, b_ref, o_ref):
    a = _silu(c_ref[...])
    o_ref[...] = jnp.dot(a, w_ref[...], preferred_element_type=F32, precision=HIGHEST) + b_ref[...]


def _adaln(cvecs, w_ada, b_ada, layer):
    rows, d = cvecs.shape
    n = w_ada.shape[-1]
    tn = 1024
    return pl.pallas_call(
        _adaln_kernel,
        grid=(n // tn,),
        in_specs=[pl.BlockSpec((rows, d), lambda j: (0, 0)),
                  pl.BlockSpec((None, d, tn), lambda j: (layer, 0, j)),
                  pl.BlockSpec((None, 1, tn), lambda j: (layer, 0, j))],
        out_specs=pl.BlockSpec((rows, tn), lambda j: (0, j)),
        out_shape=jax.ShapeDtypeStruct((rows, n), F32),
        compiler_params=_cparams(("arbitrary",)),
        name="adaln",
    )(cvecs, w_ada, b_ada.reshape(b_ada.shape[0], 1, n))


def _mm_kernel(*refs, norm, residual, tm, rc):
    it = iter(refs)
    x_ref = next(it)
    if norm:
        g_ref, sh_ref, sc_ref = next(it), next(it), next(it)
    w_ref = next(it)
    if residual:
        res_ref, gate_ref = next(it), next(it)
    o_ref = next(it)
    if norm:
        h_ref = next(it)

        @pl.when(pl.program_id(2) == 0)
        def _():
            gain = g_ref[...]
            scale1 = 1.0 + sc_ref[0, 0]
            shift = sh_ref[0, 0]

            def body(r, carry):
                xs = x_ref[0, pl.ds(pl.multiple_of(r * rc, rc), rc), :]
                ms = jnp.mean(xs * xs, axis=-1, keepdims=True)
                y = xs * lax.rsqrt(ms + EPS) * gain
                h_ref[pl.ds(pl.multiple_of(r * rc, rc), rc), :] = (y * scale1 + shift).astype(BF16)
                return carry

            lax.fori_loop(0, tm // rc, body, 0)

        lhs = h_ref[...]
    else:
        lhs = x_ref[0]
    acc = jnp.dot(lhs, w_ref[...], preferred_element_type=F32)
    if residual:
        o_ref[0] = res_ref[0] + gate_ref[0, 0] * acc
    else:
        o_ref[0] = acc.astype(o_ref.dtype)


def _mm(x, w, *, tn, out_dtype, norm=None, residual=None, name):
    g_, r_, k_ = x.shape
    n_ = w.shape[1]
    tm = min(512, r_)
    tn = _pick_tile(n_, tn)
    assert r_ % tm == 0
    grid = (g_, r_ // tm, n_ // tn)
    in_specs = [pl.BlockSpec((1, tm, k_), lambda g, i, j: (g, i, 0))]
    args = [x]
    scratch = []
    if norm is not None:
        gain, mod4, shi, sci, row0, rstr = norm
        in_specs += [pl.BlockSpec((1, k_), lambda g, i, j: (0, 0)),
                     pl.BlockSpec((1, 1, 1, k_), lambda g, i, j: (row0 + g * rstr, shi, 0, 0)),
                     pl.BlockSpec((1, 1, 1, k_), lambda g, i, j: (row0 + g * rstr, sci, 0, 0))]
        args += [gain, mod4, mod4]
        scratch = [pltpu.VMEM((tm, k_), BF16)]
    in_specs.append(pl.BlockSpec((k_, tn), lambda g, i, j: (0, j)))
    args.append(w)
    if residual is not None:
        res, mod4r, gti, rrow0, rrstr = residual
        in_specs += [pl.BlockSpec((1, tm, tn), lambda g, i, j: (g, i, j)),
                     pl.BlockSpec((1, 1, 1, tn), lambda g, i, j: (rrow0 + g * rrstr, gti, 0, j))]
        args += [res, mod4r]
    kern = functools.partial(_mm_kernel, norm=norm is not None, residual=residual is not None,
                             tm=tm, rc=min(128, tm))
    return pl.pallas_call(
        kern, grid=grid, in_specs=in_specs,
        out_specs=pl.BlockSpec((1, tm, tn), lambda g, i, j: (g, i, j)),
        out_shape=jax.ShapeDtypeStruct((g_, r_, n_), out_dtype),
        scratch_shapes=scratch,
        compiler_params=_cparams(("arbitrary", "arbitrary", "arbitrary")),
        name=name,
    )(*args)


def _qkprep_kernel(q_ref, k_ref, cos_ref, sin_ref, gq_ref, gk_ref, gm_ref, qo_ref, ko_ref, *, tm):
    cos = cos_ref[...]
    sin = sin_ref[...]
    gm = gm_ref[...]
    lane = lax.broadcasted_iota(jnp.int32, (tm, LANES), 1)
    first = (lane % DIFF_QK_DIM) < (DIFF_QK_DIM // 2)
    for src, g_ref, dst in ((q_ref, gq_ref, qo_ref), (k_ref, gk_ref, ko_ref)):
        gain = g_ref[...]
        for h in range(src.shape[-1] // LANES):
            sl = slice(h * LANES, (h + 1) * LANES)
            x = src[0, :, sl].astype(F32)
            sq = x * x
            hi = sq.astype(BF16)
            lo = (sq - hi.astype(F32)).astype(BF16)
            ss = (jnp.dot(hi, gm, preferred_element_type=F32) + jnp.dot(lo, gm, preferred_element_type=F32))
            y = x * lax.rsqrt(ss * (1.0 / DIFF_QK_DIM) + EPS) * gain
            partner = jnp.where(first, pltpu.roll(y, LANES - DIFF_QK_DIM // 2, 1),
                                pltpu.roll(y, DIFF_QK_DIM // 2, 1))
            dst[0, :, sl] = (y * cos + partner * sin).astype(BF16)


def _qkprep(u, cos_t, sin_t, gq, gk, gmat, qcb, kcb):
    b_, n_, _ = u.shape
    width = 2 * N_DIFF_HEADS * DIFF_QK_DIM
    tm = min(512, n_)
    kern = functools.partial(_qkprep_kernel, tm=tm)
    out = jax.ShapeDtypeStruct((b_, n_, width), BF16)
    return pl.pallas_call(
        kern, grid=(b_, n_ // tm),
        in_specs=[pl.BlockSpec((1, tm, width), lambda b, i: (b, i, qcb)),
                  pl.BlockSpec((1, tm, width), lambda b, i: (b, i, kcb)),
                  pl.BlockSpec((tm, LANES), lambda b, i: (i, 0)),
                  pl.BlockSpec((tm, LANES), lambda b, i: (i, 0)),
                  pl.BlockSpec((1, LANES), lambda b, i: (0, 0)),
                  pl.BlockSpec((1, LANES), lambda b, i: (0, 0)),
                  pl.BlockSpec((LANES, LANES), lambda b, i: (0, 0))],
        out_specs=[pl.BlockSpec((1, tm, width), lambda b, i: (b, i, 0)),
                   pl.BlockSpec((1, tm, width), lambda b, i: (b, i, 0))],
        out_shape=[out, out],
        compiler_params=_cparams(("arbitrary", "arbitrary")),
        name="qkprep",
    )(u, u, cos_t, sin_t, gq, gk, gmat)


def _attn_kernel(*refs, tq, kc, has_ctx):
    if has_ctx:
        q_ref, kl_ref, vl_ref, kc_ref, vc_ref, lam_ref, og_ref, o_ref, vxl_ref, vxc_ref = refs
    else:
        q_ref, kl_ref, vl_ref, lam_ref, og_ref, o_ref, vxl_ref = refs

    @pl.when(pl.program_id(2) == 0)
    def _():
        for v_ref, vx_ref in ((vl_ref, vxl_ref),) + (((vc_ref, vxc_ref),) if has_ctx else ()):
            vx_ref[:, :LANES] = v_ref[0]
            vx_ref[:, LANES:] = jnp.ones((vx_ref.shape[0], LANES), BF16)

    q = q_ref[0]
    lane = lax.broadcasted_iota(jnp.int32, (tq, LANES), 1)
    zero = jnp.zeros_like(q)
    qq = jnp.concatenate([jnp.where(lane < DIFF_QK_DIM, q, zero),
                          jnp.where(lane >= DIFF_QK_DIM, q, zero)], axis=0)
    chunks = [(kl_ref, vxl_ref, c * kc, min(kc, kl_ref.shape[1] - c * kc))
              for c in range(-(-kl_ref.shape[1] // kc))]
    if has_ctx:
        chunks += [(kc_ref, vxc_ref, c * kc, min(kc, kc_ref.shape[1] - c * kc))
                   for c in range(-(-kc_ref.shape[1] // kc))]
    m = None
    acc = None
    for k_ref, vx_ref, start, size in chunks:
        s = lax.dot_general(qq, k_ref[0, pl.ds(start, size), :], _NT, preferred_element_type=F32)
        cmax = jnp.max(s, axis=-1, keepdims=True)
        m_new = cmax if m is None else jnp.maximum(m, cmax)
        p = jnp.exp2(s - m_new).astype(BF16)
        pv = jnp.dot(p, vx_ref[pl.ds(start, size), :], preferred_element_type=F32)
        acc = pv if acc is None else acc * jnp.exp2(m - m_new) + pv
        m = m_new
    lam = lam_ref[0:1, 0:1]
    o1 = acc[:tq, :LANES] / acc[:tq, LANES:LANES + 1]
    o2 = acc[tq:, :LANES] / acc[tq:, LANES:LANES + 1]
    o = o1 - lam * o2
    o = o * lax.rsqrt(jnp.mean(o * o, axis=-1, keepdims=True) + EPS) * og_ref[...]
    o_ref[0] = o.astype(BF16)


def _attn(qp, kp, u, vcb, lam_v, og, ctx=None):
    b_, n_, _ = qp.shape
    tq = min(256, n_)
    kc = 512
    has_ctx = ctx is not None
    in_specs = [pl.BlockSpec((1, tq, LANES), lambda b, h, i: (b, i, h)),
                pl.BlockSpec((1, n_, LANES), lambda b, h, i: (b, 0, h)),
                pl.BlockSpec((1, n_, LANES), lambda b, h, i: (b, 0, vcb + h))]
    args = [qp, kp, u]
    if has_ctx:
        kpc, uc = ctx
        nc = kpc.shape[1]
        in_specs += [pl.BlockSpec((1, nc, LANES), lambda b, h, i: (b, 0, h)),
                     pl.BlockSpec((1, nc, LANES), lambda b, h, i: (b, 0, vcb + h))]
        args += [kpc, uc]
    in_specs += [pl.BlockSpec((1, LANES), lambda b, h, i: (0, 0)),
                 pl.BlockSpec((1, LANES), lambda b, h, i: (0, 0))]
    args += [lam_v, og]
    scratch = [pltpu.VMEM((n_, 2 * LANES), BF16)]
    if has_ctx:
        scratch.append(pltpu.VMEM((ctx[0].shape[1], 2 * LANES), BF16))
    kern = functools.partial(_attn_kernel, tq=tq, kc=kc, has_ctx=has_ctx)
    return pl.pallas_call(
        kern, grid=(b_, N_DIFF_HEADS, n_ // tq), in_specs=in_specs,
        out_specs=pl.BlockSpec((1, tq, LANES), lambda b, h, i: (b, i, h)),
        out_shape=jax.ShapeDtypeStruct((b_, n_, N_DIFF_HEADS * LANES), BF16),
        scratch_shapes=scratch,
        compiler_params=_cparams(("arbitrary", "arbitrary", "arbitrary")),
        name="diff_attn_ctx" if has_ctx else "diff_attn",
    )(*args)


def _gla_constants(tt, reverse):
    c = GLA_CHUNK
    idx = np.arange(tt)
    same_chunk = (idx[:, None] // c) == (idx[None, :] // c)
    if reverse:
        cum = same_chunk & (idx[None, :] >= idx[:, None])
    else:
        cum = same_chunk & (idx[None, :] <= idx[:, None])
    tot = same_chunk
    sels, masks = [], []
    s = c // 2
    while s >= 1:
        blk = idx // (2 * s)
        right = (idx % (2 * s)) >= s
        same = blk[:, None] == blk[None, :]
        if reverse:
            ref = blk * 2 * s + s
            mask = same & (~right)[:, None] & right[None, :]
        else:
            ref = blk * 2 * s + s - 1
            mask = same & right[:, None] & (~right)[None, :]
        sels.append(idx[None, :] == ref[:, None])
        masks.append(mask)
        s //= 2
    masks.append(idx[:, None] == idx[None, :])
    cum_f = cum.astype(np.float32)
    dall = np.concatenate([cum_f, tot.astype(np.float32)]
                          + [cum_f - sel.astype(np.float32) @ cum_f for sel in sels], axis=0)
    return jnp.asarray(dall).astype(BF16), jnp.asarray(np.stack(masks).astype(np.float32))


def _gla_kernel(q_ref, k_ref, v_ref, lr_ref, w2h_ref, w2l_ref, b_ref, dall_ref, mask_ref, s0_ref,
                o_ref, sfin_ref, s_ref, *, tt, reverse, hk, hv):
    t = pl.program_id(1)

    @pl.when(t == 0)
    def _():
        s_ref[...] = s0_ref[0]

    n_lvl = mask_ref.shape[0] - 1
    lr = lr_ref[0]
    z = (jnp.dot(lr, w2h_ref[...], preferred_element_type=F32)
         + jnp.dot(lr, w2l_ref[...], preferred_element_type=F32) + b_ref[...])
    g = (jnp.minimum(z, 0.0) - jnp.log(1.0 + jnp.exp(-jnp.abs(z)))) * (1.0 / GATE_TAU)
    g_hi = g.astype(BF16)
    g_lo = (g - g_hi.astype(F32)).astype(BF16)
    sums = (jnp.dot(dall_ref[...], g_hi, preferred_element_type=F32)
            + jnp.dot(dall_ref[...], g_lo, preferred_element_type=F32))
    cum = sums[0:tt]
    tot = sums[tt:2 * tt]
    q = q_ref[0].astype(F32) * (hk ** -0.5)
    k = k_ref[0].astype(F32)
    eq = jnp.exp(cum)
    ek = jnp.exp(tot - cum)
    nch = tt // GLA_CHUNK
    order = range(nch - 1, -1, -1) if reverse else range(nch)
    for h in range(N_GLA_HEADS):
        sl = slice(h * hk, (h + 1) * hk)
        qh, kh = q[:, sl], k[:, sl]
        vh = v_ref[0, :, h * hv:(h + 1) * hv]
        a = mask_ref[n_lvl] * lax.dot_general(qh.astype(BF16), kh.astype(BF16), _NT, preferred_element_type=F32)
        for l in range(n_lvl):
            d = sums[(2 + l) * tt:(3 + l) * tt, sl]
            qt = (qh * jnp.exp(jnp.minimum(d, 0.0))).astype(BF16)
            kt = (kh * jnp.exp(jnp.minimum(-d, 0.0))).astype(BF16)
            a = a + mask_ref[l] * lax.dot_general(qt, kt, _NT, preferred_element_type=F32)
        o_h = jnp.dot(a.astype(BF16), vh, preferred_element_type=F32)
        qe = (qh * eq[:, sl]).astype(BF16)
        kd = (kh * ek[:, sl]).astype(BF16)
        outs = [None] * nch
        for c in order:
            rows = slice(c * GLA_CHUNK, (c + 1) * GLA_CHUNK)
            st = s_ref[h]
            outs[c] = o_h[rows] + lax.dot_general(qe[rows], st.astype(BF16), _NT, preferred_element_type=F32)
            dec = jnp.exp(tot[c * GLA_CHUNK:c * GLA_CHUNK + 1, sl])
            s_ref[h] = st * dec + lax.dot_general(vh[rows], kd[rows], _TN, preferred_element_type=F32)
        o_ref[0, :, h * hv:(h + 1) * hv] = jnp.concatenate(outs, axis=0) if nch > 1 else outs[0]

    @pl.when(t == pl.num_programs(1) - 1)
    def _():
        sfin_ref[0] = s_ref[...]


def _gla_dir(u, cbs, w2p, balpha, s0, reverse):
    b_, n_, _ = u.shape
    qcb, kcb, vcb, lrcb, dk, dv = cbs
    hk, hv = dk // N_GLA_HEADS, dv // N_GLA_HEADS
    tt = min(128, n_)
    nt = n_ // tt
    dall, mask = _gla_constants(tt, reverse)
    w2h = w2p.astype(BF16)
    w2l = (w2p - w2h.astype(F32)).astype(BF16)
    tmap = (lambda t: nt - 1 - t) if reverse else (lambda t: t)
    c2 = lambda b, t: (0, 0)
    c3 = lambda b, t: (0, 0, 0)
    kern = functools.partial(_gla_kernel, tt=tt, reverse=reverse, hk=hk, hv=hv)
    return pl.pallas_call(
        kern, grid=(b_, nt),
        in_specs=[pl.BlockSpec((1, tt, dk), lambda b, t: (b, tmap(t), qcb)),
                  pl.BlockSpec((1, tt, dk), lambda b, t: (b, tmap(t), kcb)),
                  pl.BlockSpec((1, tt, dv), lambda b, t: (b, tmap(t), vcb)),
                  pl.BlockSpec((1, tt, LANES), lambda b, t: (b, tmap(t), lrcb)),
                  pl.BlockSpec((LANES, dk), c2),
                  pl.BlockSpec((LANES, dk), c2),
                  pl.BlockSpec((1, dk), c2),
                  pl.BlockSpec(dall.shape, c2),
                  pl.BlockSpec(mask.shape, c3),
                  pl.BlockSpec((1, N_GLA_HEADS, hv, hk), lambda b, t: (b, 0, 0, 0))],
        out_specs=[pl.BlockSpec((1, tt, dv), lambda b, t: (b, tmap(t), 0)),
                   pl.BlockSpec((1, N_GLA_HEADS, hv, hk), lambda b, t: (b, 0, 0, 0))],
        out_shape=[jax.ShapeDtypeStruct((b_, n_, dv), F32),
                   jax.ShapeDtypeStruct((b_, N_GLA_HEADS, hv, hk), F32)],
        scratch_shapes=[pltpu.VMEM((N_GLA_HEADS, hv, hk), F32)],
        compiler_params=_cparams(("arbitrary", "arbitrary")),
        name="gla_bwd" if reverse else "gla_fwd",
    )(u, u, u, u, w2h, w2l, balpha, dall, mask, s0)


def _gla_out_kernel(of_ref, ob_ref, r_ref, gn_ref, o_ref, *, hv):
    o = of_ref[0] + ob_ref[0]
    r = r_ref[0].astype(F32)
    gate = _silu(r)
    gn = gn_ref[...]
    for h in range(o.shape[-1] // hv):
        sl = slice(h * hv, (h + 1) * hv)
        oh = o[:, sl]
        y = oh * lax.rsqrt(jnp.mean(oh * oh, axis=-1, keepdims=True) + EPS) * gn
        o_ref[0, :, sl] = (y * gate[:, sl]).astype(BF16)


def _gla_out(o_f, o_b, u, rcb, gn):
    b_, n_, dv = o_f.shape
    tm = min(512, n_)
    kern = functools.partial(_gla_out_kernel, hv=dv // N_GLA_HEADS)
    blk = lambda cb: pl.BlockSpec((1, tm, dv), lambda b, i: (b, i, cb))
    return pl.pallas_call(
        kern, grid=(b_, n_ // tm),
        in_specs=[blk(0), blk(0), blk(rcb), pl.BlockSpec((1, dv // N_GLA_HEADS), lambda b, i: (0, 0))],
        out_specs=blk(0),
        out_shape=jax.ShapeDtypeStruct((b_, n_, dv), BF16),
        compiler_params=_cparams(("arbitrary", "arbitrary")),
        name="gla_out",
    )(o_f, o_b, u, gn)


HALO = 16


def _conformer_kernel(a_ref, p_ref, n_ref, cw_ref, cb_ref, lg_ref, lb_ref, o_ref, h_ref, *, tm, da):
    i = pl.program_id(1)
    last = pl.num_programs(1) - 1

    def glu(a):
        a = a.astype(F32)
        return a[:, :da] * _sigmoid(a[:, da:])

    h_ref[pl.ds(0, HALO), :] = jnp.where(i > 0, glu(p_ref[0]), 0.0)
    h_ref[pl.ds(HALO, tm), :] = glu(a_ref[0])
    h_ref[pl.ds(HALO + tm, HALO), :] = jnp.where(i < last, glu(n_ref[0]), 0.0)
    pad = (CONV_A_W - 1) // 2
    acc = jnp.zeros((tm, da), F32) + cb_ref[...]
    for k in range(CONV_A_W):
        acc = acc + cw_ref[k:k + 1, :] * h_ref[pl.ds(HALO - pad + k, tm), :]
    mu = jnp.mean(acc, axis=-1, keepdims=True)
    xc = acc - mu
    var = jnp.mean(xc * xc, axis=-1, keepdims=True)
    y = xc * lax.rsqrt(var + EPS) * lg_ref[...] + lb_ref[...]
    o_ref[0] = _silu(y).astype(BF16)


def _conformer(u, acb, conv_w, conv_b, ln_g, ln_b):
    b_, n_, _ = u.shape
    da = conv_w.shape[-1]
    tm = min(256, n_)
    nh = tm // HALO
    nblk = n_ // HALO
    kern = functools.partial(_conformer_kernel, tm=tm, da=da)
    row = lambda b, i: (0, 0)
    return pl.pallas_call(
        kern, grid=(b_, n_ // tm),
        in_specs=[pl.BlockSpec((1, tm, 2 * da), lambda b, i: (b, i, acb)),
                  pl.BlockSpec((1, HALO, 2 * da), lambda b, i: (b, jnp.maximum(i * nh - 1, 0), acb)),
                  pl.BlockSpec((1, HALO, 2 * da), lambda b, i: (b, jnp.minimum((i + 1) * nh, nblk - 1), acb)),
                  pl.BlockSpec((CONV_A_W, da), row),
                  pl.BlockSpec((1, da), row), pl.BlockSpec((1, da), row), pl.BlockSpec((1, da), row)],
        out_specs=pl.BlockSpec((1, tm, da), lambda b, i: (b, i, 0)),
        out_shape=jax.ShapeDtypeStruct((b_, n_, da), BF16),
        scratch_shapes=[pltpu.VMEM((tm + 2 * HALO, da), F32)],
        compiler_params=_cparams(("arbitrary", "arbitrary")),
        name="conformer",
    )(u, u, u, conv_w, conv_b, ln_g, ln_b)


def _merge_kernel(ha_ref, hb_ref, hc_ref, ga_ref, gb_ref, gc_ref, wa_ref, wb_ref, wc_ref, bg_ref, o_ref):
    acc = None
    for idx, (h_ref, g_ref, w_ref) in enumerate(((ha_ref, ga_ref, wa_ref), (hb_ref, gb_ref, wb_ref),
                                                 (hc_ref, gc_ref, wc_ref))):
        y = jnp.dot(h_ref[0], w_ref[...], preferred_element_type=F32)
        gate = _sigmoid(g_ref[0].astype(F32) + bg_ref[idx])
        acc = gate * y if acc is None else acc + gate * y
    o_ref[0] = acc.astype(BF16)


def _merge(ha, hb, hc, u, wa, wb, wc, bgate):
    b_, n_, kd = ha.shape
    d = wa.shape[1]
    tm = min(512, n_)
    tn = 512
    nj = d // tn
    hspec = pl.BlockSpec((1, tm, kd), lambda b, i, j: (b, i, 0))
    gspec = lambda br: pl.BlockSpec((1, tm, tn), lambda b, i, j: (b, i, br * nj + j))
    wspec = pl.BlockSpec((kd, tn), lambda b, i, j: (0, j))
    return pl.pallas_call(
        _merge_kernel, grid=(b_, n_ // tm, nj),
        in_specs=[hspec, hspec, hspec, gspec(0), gspec(1), gspec(2), wspec, wspec, wspec,
                  pl.BlockSpec((3, 1, tn), lambda b, i, j: (0, 0, j))],
        out_specs=pl.BlockSpec((1, tm, tn), lambda b, i, j: (b, i, j)),
        out_shape=jax.ShapeDtypeStruct((b_, n_, d), BF16),
        compiler_params=_cparams(("arbitrary", "arbitrary", "arbitrary")),
        name="merge",
    )(ha, hb, hc, u, u, u, wa, wb, wc, bgate)


def _ffn_down_kernel(uv_ref, ug_ref, pv_ref, nv_ref, pg_ref, ng_ref, cwv_ref, cwg_ref, cbv_ref, cbg_ref,
                     wd_ref, res_ref, gate_ref, o_ref, acc_ref, *, tm, tf):
    i = pl.program_id(1)
    j = pl.program_id(2)
    last_i = pl.num_programs(1) - 1
    row = lax.broadcasted_iota(jnp.int32, (tm, tf), 0)

    def conv(u_ref, p_ref, n_ref, cw_ref, cb_ref):
        u = u_ref[0].astype(F32)
        prev = jnp.where(i > 0, p_ref[0].astype(F32)[HALO - 1:HALO, :], 0.0)
        nxt = jnp.where(i < last_i, n_ref[0].astype(F32)[0:1, :], 0.0)
        up = jnp.where(row == 0, prev, pltpu.roll(u, 1, 0))
        dn = jnp.where(row == tm - 1, nxt, pltpu.roll(u, tm - 1, 0))
        return cw_ref[0:1, :] * up + cw_ref[1:2, :] * u + cw_ref[2:3, :] * dn + cb_ref[...]

    val = conv(uv_ref, pv_ref, nv_ref, cwv_ref, cbv_ref)
    gt = conv(ug_ref, pg_ref, ng_ref, cwg_ref, cbg_ref)
    act = (_silu(gt) * val).astype(BF16)
    part = jnp.dot(act, wd_ref[...], preferred_element_type=F32)

    @pl.when(j == 0)
    def _():
        acc_ref[...] = part

    @pl.when(j > 0)
    def _():
        acc_ref[...] += part

    @pl.when(j == pl.num_programs(2) - 1)
    def _():
        o_ref[0] = res_ref[0] + gate_ref[0, 0] * acc_ref[...]


def _ffn_down(uf, conv_w, conv_b, wd, res, mod4, gti, row0, rstr):
    b_, n_, f2 = uf.shape
    dff = f2 // 2
    d = wd.shape[1]
    tm = min(512, n_)
    tf = _pick_tile(dff, 1408)
    nf = dff // tf
    nh = tm // HALO
    nblk = n_ // HALO
    kern = functools.partial(_ffn_down_kernel, tm=tm, tf=tf)
    main = lambda off: pl.BlockSpec((1, tm, tf), lambda b, i, j: (b, i, off + j))
    prev = lambda off: pl.BlockSpec((1, HALO, tf), lambda b, i, j: (b, jnp.maximum(i * nh - 1, 0), off + j))
    nxt = lambda off: pl.BlockSpec((1, HALO, tf), lambda b, i, j: (b, jnp.minimum((i + 1) * nh, nblk - 1), off + j))
    cw = lambda off: pl.BlockSpec((3, tf), lambda b, i, j: (0, off + j))
    cb = lambda off: pl.BlockSpec((1, tf), lambda b, i, j: (0, off + j))
    return pl.pallas_call(
        kern, grid=(b_, n_ // tm, nf),
        in_specs=[main(0), main(nf), prev(0), nxt(0), prev(nf), nxt(nf), cw(0), cw(nf), cb(0), cb(nf),
                  pl.BlockSpec((tf, d), lambda b, i, j: (j, 0)),
                  pl.BlockSpec((1, tm, d), lambda b, i, j: (b, i, 0)),
                  pl.BlockSpec((1, 1, 1, d), lambda b, i, j: (row0 + b * rstr, gti, 0, 0))],
        out_specs=pl.BlockSpec((1, tm, d), lambda b, i, j: (b, i, 0)),
        out_shape=jax.ShapeDtypeStruct((b_, n_, d), F32),
        scratch_shapes=[pltpu.VMEM((tm, d), F32)],
        compiler_params=_cparams(("arbitrary", "arbitrary", "arbitrary")),
        name="ffn_down",
    )(uf, uf, uf, uf, uf, uf, conv_w, conv_w, conv_b, conv_b, wd, res, mod4)


def _rope_tables(n_tokens):
    rows = n_tokens // GRID_W
    row = jnp.broadcast_to(jnp.arange(rows, dtype=F32)[:, None], (rows, GRID_W)).reshape(-1)
    col = jnp.broadcast_to(jnp.arange(GRID_W, dtype=F32)[None, :], (rows, GRID_W)).reshape(-1)
    n_freq = DIFF_QK_DIM // 4
    inv = ROPE_BASE ** (-jnp.arange(n_freq, dtype=F32) / n_freq)
    ang = jnp.concatenate([row[:, None] * inv, col[:, None] * inv], axis=-1)
    cos, sin = jnp.cos(ang), jnp.sin(ang)
    cos_t = jnp.tile(cos, (1, LANES // cos.shape[1]))
    sin_t = jnp.tile(jnp.concatenate([-sin, sin], axis=-1), (1, LANES // (2 * sin.shape[1])))
    return cos_t, sin_t


def kernel(x, c, ctx, c_ctx, w_ada, b_ada, g_norm1, w_in, b_gate, conv_a_w, conv_a_b, ln_a_g, ln_a_b, w_a_out,
           qn_g, kn_g, lam_q1, lam_k1, lam_q2, lam_k2, subln_g, w_b_out, w_alpha2, b_alpha, gn_c_g, w_c_out,
           w_o, g_norm2, w_up, conv_f_w, conv_f_b, w_down):
    b_, n_lat, d = x.shape
    n_ctx = ctx.shape[1]
    depth = w_ada.shape[0]
    da = conv_a_w.shape[-1]
    qk_w = 2 * N_DIFF_HEADS * DIFF_QK_DIM
    db = w_b_out.shape[1]
    dc = w_c_out.shape[1]
    dk = w_alpha2.shape[-1]
    dff = w_down.shape[1]

    sizes = (2 * da, qk_w, qk_w, db, dk, dk, dc, dc, 2 * GATE_RANK, 3 * d)
    offs = np.concatenate([[0], np.cumsum(sizes)])
    o_a, o_qd, o_kd, o_vd, o_qg, o_kg, o_vg, o_r, o_lr, o_gate = (int(v) for v in offs[:-1])
    new_order = [(o_gate, 3 * d), (o_a, 2 * da), (o_qd, qk_w), (o_kd, qk_w), (o_vd, db), (o_vg, dc), (o_r, dc),
                 (o_qg, dk), (o_kg, dk), (o_lr, 2 * GATE_RANK)]
    starts = np.concatenate([[0], np.cumsum([s for _, s in new_order])])
    p_gate, p_a, p_qd, p_kd, p_vd, p_vg, p_r, p_qg, p_kg, p_lr = (int(v) for v in starts[:-1])
    tn_in = 768
    np_cols = -(-(p_lr + LANES) // tn_in) * tn_in
    assert p_a % (2 * da) == 0 and p_qd % qk_w == 0 and p_kd % qk_w == 0 and p_vd % LANES == 0
    assert p_vg % dc == 0 and p_r % dc == 0 and p_qg % dk == 0 and p_kg % dk == 0 and p_lr % LANES == 0

    def relayout_w_in(w):
        parts = [w[:, o:o + s] for o, s in new_order]
        parts.append(jnp.zeros((w.shape[0], np_cols - (p_lr + 2 * GATE_RANK)), w.dtype))
        return jnp.concatenate(parts, axis=1).astype(BF16)

    cos_l, sin_l = _rope_tables(n_lat)
    cos_c = jnp.ones((n_ctx, LANES), F32)
    sin_c = jnp.zeros((n_ctx, LANES), F32)
    grp = np.arange(LANES) // DIFF_QK_DIM
    gmat = jnp.asarray((grp[:, None] == grp[None, :]).astype(np.float32)).astype(BF16)

    cvecs = jnp.zeros((8, d), F32).at[:b_].set(c).at[b_].set(c_ctx)
    ctx_row = b_

    s_zero = jnp.zeros((b_, N_GLA_HEADS, dc // N_GLA_HEADS, dk // N_GLA_HEADS), F32)

    for l in range(depth):
        last = l == depth - 1
        lam_init = 0.8 - 0.6 * math.exp(-0.3 * l)
        lam = (jnp.exp(jnp.sum(lam_q1[l] * lam_k1[l])) - jnp.exp(jnp.sum(lam_q2[l] * lam_k2[l])) + lam_init)
        lam_v = jnp.full((1, LANES), lam, F32)
        og = (subln_g[l] * (1.0 - lam_init)).reshape(1, LANES)
        gq = (jnp.tile(qn_g[l], LANES // DIFF_QK_DIM) * (DIFF_QK_DIM ** -0.5 * math.log2(math.e))).reshape(1, LANES)
        gk = jnp.tile(kn_g[l], LANES // DIFF_QK_DIM).reshape(1, LANES)

        mod = _adaln(cvecs, w_ada, b_ada, l)
        mod4 = mod.reshape(8, 6, 1, d)

        w_in_p = relayout_w_in(w_in[l])
        wa, wb, wc = w_a_out[l].astype(BF16), w_b_out[l].astype(BF16), w_c_out[l].astype(BF16)
        wo = w_o[l].astype(BF16)
        wup = w_up[l].astype(BF16)
        wdn = w_down[l].astype(BF16)
        g1 = g_norm1[l].reshape(1, d)
        g2 = g_norm2[l].reshape(1, d)
        bg = b_gate[l].reshape(3, 1, d)
        w2 = w_alpha2[l]
        w2p = [jnp.zeros((LANES, dk), F32).at[z * GATE_RANK:(z + 1) * GATE_RANK].set(w2[z]) for z in range(2)]
        bal = [b_alpha[l, z].reshape(1, dk) for z in range(2)]
        gla_cbs = (p_qg // dk, p_kg // dk, p_vg // dc, p_lr // LANES, dk, dc)

        u_l = _mm(x, w_in_p, tn=2432, out_dtype=BF16, norm=(g1, mod4, 0, 1, 0, 1), name="in_proj")
        ctx_flat = ctx.reshape(1, b_ * n_ctx, d)
        u_c = _mm(ctx_flat, w_in_p, tn=2432, out_dtype=BF16, norm=(g1, mod4, 0, 1, ctx_row, 0),
                  name="in_proj_ctx").reshape(b_, n_ctx, np_cols)

        qp_l, kp_l = _qkprep(u_l, cos_l, sin_l, gq, gk, gmat, p_qd // qk_w, p_kd // qk_w)
        qp_c, kp_c = _qkprep(u_c, cos_c, sin_c, gq, gk, gmat, p_qd // qk_w, p_kd // qk_w)
        hb_l = _attn(qp_l, kp_l, u_l, p_vd // LANES, lam_v, og, ctx=(kp_c, u_c))

        of_c, sf = _gla_dir(u_c, gla_cbs, w2p[0], bal[0], s_zero, False)
        ob_c, sb = _gla_dir(u_c, gla_cbs, w2p[1], bal[1], s_zero, True)
        of_l, _ = _gla_dir(u_l, gla_cbs, w2p[0], bal[0], sf, False)
        ob_l, _ = _gla_dir(u_l, gla_cbs, w2p[1], bal[1], sb, True)
        gn = gn_c_g[l].reshape(1, -1)
        hc_l = _gla_out(of_l, ob_l, u_l, p_r // dc, gn)

        cb_a, lg_a, lb_a = conv_a_b[l].reshape(1, da), ln_a_g[l].reshape(1, da), ln_a_b[l].reshape(1, da)
        ha_l = _conformer(u_l, p_a // (2 * da), conv_a_w[l], cb_a, lg_a, lb_a)

        m_l = _merge(ha_l, hb_l, hc_l, u_l, wa, wb, wc, bg)
        x_mid = _mm(m_l, wo, tn=2048, out_dtype=F32, residual=(x, mod4, 2, 0, 1), name="out_proj")

        cfb = conv_f_b[l].reshape(1, 2 * dff)
        uf_l = _mm(x_mid, wup, tn=2816, out_dtype=BF16, norm=(g2, mod4, 3, 4, 0, 1), name="ffn_up")
        x_new = _ffn_down(uf_l, conv_f_w[l], cfb, wdn, x_mid, mod4, 5, 0, 1)

        if not last:
            hb_c = _attn(qp_c, kp_c, u_c, p_vd // LANES, lam_v, og)
            hc_c = _gla_out(of_c, ob_c, u_c, p_r // dc, gn)
            ha_c = _conformer(u_c, p_a // (2 * da), conv_a_w[l], cb_a, lg_a, lb_a)
            m_c = _merge(ha_c, hb_c, hc_c, u_c, wa, wb, wc, bg)
            ctx_mid = _mm(m_c.reshape(1, b_ * n_ctx, d), wo, tn=2048, out_dtype=F32,
                          residual=(ctx_flat, mod4, 2, ctx_row, 0), name="out_proj_ctx")
            uf_c = _mm(ctx_mid, wup, tn=2816, out_dtype=BF16, norm=(g2, mod4, 3, 4, ctx_row, 0),
                       name="ffn_up_ctx").reshape(b_, n_ctx, 2 * dff)
            ctx = _ffn_down(uf_c, conv_f_w[l], cfb, wdn, ctx_mid.reshape(b_, n_ctx, d), mod4, 5, ctx_row, 0)

        x = x_new

    return x
```

```python
import functools
import math

import numpy as np
import jax
import jax.numpy as jnp
from jax import lax
from jax.experimental import pallas as pl
from jax.experimental.pallas import tpu as pltpu

F32 = jnp.float32
BF16 = jnp.bfloat16
HIGHEST = lax.Precision.HIGHEST

EPS = 1e-6
GRID_W = 64
ROPE_BASE = 10000.0
N_DIFF_HEADS = 8
DIFF_QK_DIM = 64
N_GLA_HEADS = 4
GATE_RANK = 16
GATE_TAU = 16.0
GLA_CHUNK = 64
CONV_A_W = 31

LANES = 128
VMEM_LIMIT_BYTES = 56 * 1024 * 1024

_NT = (((1,), (1,)), ((), ()))
_TN = (((0,), (0,)), ((), ()))


def _cparams(sem):
    return pltpu.CompilerParams(dimension_semantics=sem, vmem_limit_bytes=VMEM_LIMIT_BYTES)


def _pick_tile(total, target):
    best = LANES
    for t in range(LANES, min(total, target) + 1, LANES):
        if total % t == 0:
            best = t
    return best


def _sigmoid(x):
    return 1.0 / (1.0 + jnp.exp(-x))


def _silu(x):
    return x * _sigmoid(x)


def _adaln_kernel(c_ref, w_ref, b_ref, o_ref):
    a = _silu(c_ref[...])
    o_ref[...] = jnp.dot(a, w_ref[...], preferred_element_type=F32, precision=HIGHEST) + b_ref[...]


def _adaln(cvecs, w_ada, b_ada, layer):
    rows, d = cvecs.shape
    n = w_ada.shape[-1]
    tn = 1024
    return pl.pallas_call(
        _adaln_kernel,
        grid=(n // tn,),
        in_specs=[pl.BlockSpec((rows, d), lambda j: (0, 0)),
                  pl.BlockSpec((None, d, tn), lambda j: (layer, 0, j)),
                  pl.BlockSpec((None, 1, tn), lambda j: (layer, 0, j))],
        out_specs=pl.BlockSpec((rows, tn), lambda j: (0, j)),
        out_shape=jax.ShapeDtypeStruct((rows, n), F32),
        compiler_params=_cparams(("arbitrary",)),
        name="adaln",
    )(cvecs, w_ada, b_ada.reshape(b_ada.shape[0], 1, n))


def _mm_kernel(*refs, norm, residual, tm, rc):
    it = iter(refs)
    x_ref = next(it)
    if norm:
        g_ref, sh_ref, sc_ref = next(it), next(it), next(it)
    w_ref = next(it)
    if residual:
        res_ref, gate_ref = next(it), next(it)
    o_ref = next(it)
    if norm:
        h_ref = next(it)

        @pl.when(pl.program_id(2) == 0)
        def _():
            gain = g_ref[...]
            scale1 = 1.0 + sc_ref[0, 0]
            shift = sh_ref[0, 0]

            def body(r, carry):
                xs = x_ref[0, pl.ds(pl.multiple_of(r * rc, rc), rc), :]
                ms = jnp.mean(xs * xs, axis=-1, keepdims=True)
                y = xs * lax.rsqrt(ms + EPS) * gain
                h_ref[pl.ds(pl.multiple_of(r * rc, rc), rc), :] = (y * scale1 + shift).astype(BF16)
                return carry

            lax.fori_loop(0, tm // rc, body, 0)

        lhs = h_ref[...]
    else:
        lhs = x_ref[0]
    acc = jnp.dot(lhs, w_ref[...], preferred_element_type=F32)
    if residual:
        o_ref[0] = res_ref[0] + gate_ref[0, 0] * acc
    else:
        o_ref[0] = acc.astype(o_ref.dtype)


def _mm(x, w, *, tn, out_dtype, norm=None, residual=None, layer=None, tm=512, name):
    g_, r_, k_ = x.shape
    n_ = w.shape[-1]
    tm = min(tm, r_)
    tn = _pick_tile(n_, tn)
    assert r_ % tm == 0
    grid = (g_, r_ // tm, n_ // tn)
    in_specs = [pl.BlockSpec((1, tm, k_), lambda g, i, j: (g, i, 0))]
    args = [x]
    scratch = []
    if norm is not None:
        gain, mod4, shi, sci, row0, rstr = norm
        in_specs += [pl.BlockSpec((1, k_), lambda g, i, j: (0, 0)),
                     pl.BlockSpec((1, 1, 1, k_), lambda g, i, j: (row0 + g * rstr, shi, 0, 0)),
                     pl.BlockSpec((1, 1, 1, k_), lambda g, i, j: (row0 + g * rstr, sci, 0, 0))]
        args += [gain, mod4, mod4]
        scratch = [pltpu.VMEM((tm, k_), BF16)]
    if layer is None:
        in_specs.append(pl.BlockSpec((k_, tn), lambda g, i, j: (0, j)))
    else:
        in_specs.append(pl.BlockSpec((None, k_, tn), lambda g, i, j: (layer, 0, j)))
    args.append(w)
    if residual is not None:
        res, mod4r, gti, rrow0, rrstr = residual
        in_specs += [pl.BlockSpec((1, tm, tn), lambda g, i, j: (g, i, j)),
                     pl.BlockSpec((1, 1, 1, tn), lambda g, i, j: (rrow0 + g * rrstr, gti, 0, j))]
        args += [res, mod4r]
    kern = functools.partial(_mm_kernel, norm=norm is not None, residual=residual is not None,
                             tm=tm, rc=min(128, tm))
    return pl.pallas_call(
        kern, grid=grid, in_specs=in_specs,
        out_specs=pl.BlockSpec((1, tm, tn), lambda g, i, j: (g, i, j)),
        out_shape=jax.ShapeDtypeStruct((g_, r_, n_), out_dtype),
        scratch_shapes=scratch,
        compiler_params=_cparams(("arbitrary", "arbitrary", "arbitrary")),
        name=name,
    )(*args)


def _qkprep_kernel(q_ref, k_ref, cos_ref, sin_ref, gq_ref, gk_ref, gm_ref, qo_ref, ko_ref, *, tm):
    cos = cos_ref[...]
    sin = sin_ref[...]
    gm = gm_ref[...]
    lane = lax.broadcasted_iota(jnp.int32, (tm, LANES), 1)
    first = (lane % DIFF_QK_DIM) < (DIFF_QK_DIM // 2)
    for src, g_ref, dst in ((q_ref, gq_ref, qo_ref), (k_ref, gk_ref, ko_ref)):
        gain = g_ref[...]
        for h in range(src.shape[-1] // LANES):
            sl = slice(h * LANES, (h + 1) * LANES)
            x = src[0, :, sl].astype(F32)
            sq = x * x
            hi = sq.astype(BF16)
            lo = (sq - hi.astype(F32)).astype(BF16)
            ss = (jnp.dot(hi, gm, preferred_element_type=F32) + jnp.dot(lo, gm, preferred_element_type=F32))
            y = x * lax.rsqrt(ss * (1.0 / DIFF_QK_DIM) + EPS) * gain
            partner = jnp.where(first, pltpu.roll(y, LANES - DIFF_QK_DIM // 2, 1),
                                pltpu.roll(y, DIFF_QK_DIM // 2, 1))
            dst[0, :, sl] = (y * cos + partner * sin).astype(BF16)


def _qkprep(u, cos_t, sin_t, gq, gk, gmat, qcb, kcb):
    b_, n_, _ = u.shape
    width = 2 * N_DIFF_HEADS * DIFF_QK_DIM
    tm = min(512, n_)
    kern = functools.partial(_qkprep_kernel, tm=tm)
    out = jax.ShapeDtypeStruct((b_, n_, width), BF16)
    return pl.pallas_call(
        kern, grid=(b_, n_ // tm),
        in_specs=[pl.BlockSpec((1, tm, width), lambda b, i: (b, i, qcb)),
                  pl.BlockSpec((1, tm, width), lambda b, i: (b, i, kcb)),
                  pl.BlockSpec((tm, LANES), lambda b, i: (i, 0)),
                  pl.BlockSpec((tm, LANES), lambda b, i: (i, 0)),
                  pl.BlockSpec((1, LANES), lambda b, i: (0, 0)),
                  pl.BlockSpec((1, LANES), lambda b, i: (0, 0)),
                  pl.BlockSpec((LANES, LANES), lambda b, i: (0, 0))],
        out_specs=[pl.BlockSpec((1, tm, width), lambda b, i: (b, i, 0)),
                   pl.BlockSpec((1, tm, width), lambda b, i: (b, i, 0))],
        out_shape=[out, out],
        compiler_params=_cparams(("arbitrary", "arbitrary")),
        name="qkprep",
    )(u, u, cos_t, sin_t, gq, gk, gmat)


def _attn_kernel(*refs, tq, kc, has_ctx):
    if has_ctx:
        q_ref, kl_ref, vl_ref, kc_ref, vc_ref, lam_ref, og_ref, o_ref, vxl_ref, vxc_ref = refs
    else:
        q_ref, kl_ref, vl_ref, lam_ref, og_ref, o_ref, vxl_ref = refs

    @pl.when(pl.program_id(2) == 0)
    def _():
        for v_ref, vx_ref in ((vl_ref, vxl_ref),) + (((vc_ref, vxc_ref),) if has_ctx else ()):
            vx_ref[:, :LANES] = v_ref[0]
            vx_ref[:, LANES:] = jnp.ones((vx_ref.shape[0], LANES), BF16)

    q = q_ref[0]
    lane = lax.broadcasted_iota(jnp.int32, (tq, LANES), 1)
    zero = jnp.zeros_like(q)
    qq = jnp.concatenate([jnp.where(lane < DIFF_QK_DIM, q, zero),
                          jnp.where(lane >= DIFF_QK_DIM, q, zero)], axis=0)
    chunks = [(kl_ref, vxl_ref, c * kc, min(kc, kl_ref.shape[1] - c * kc))
              for c in range(-(-kl_ref.shape[1] // kc))]
    if has_ctx:
        chunks += [(kc_ref, vxc_ref, c * kc, min(kc, kc_ref.shape[1] - c * kc))
                   for c in range(-(-kc_ref.shape[1] // kc))]
    m = None
    acc = None
    for k_ref, vx_ref, start, size in chunks:
        s = lax.dot_general(qq, k_ref[0, pl.ds(start, size), :], _NT, preferred_element_type=F32)
        cmax = jnp.max(s, axis=-1, keepdims=True)
        m_new = cmax if m is None else jnp.maximum(m, cmax)
        p = jnp.exp2(s - m_new).astype(BF16)
        pv = jnp.dot(p, vx_ref[pl.ds(start, size), :], preferred_element_type=F32)
        acc = pv if acc is None else acc * jnp.exp2(m - m_new) + pv
        m = m_new
    lam = lam_ref[0:1, 0:1]
    o1 = acc[:tq, :LANES] / acc[:tq, LANES:LANES + 1]
    o2 = acc[tq:, :LANES] / acc[tq:, LANES:LANES + 1]
    o = o1 - lam * o2
    o = o * lax.rsqrt(jnp.mean(o * o, axis=-1, keepdims=True) + EPS) * og_ref[...]
    o_ref[0] = o.astype(BF16)


def _attn(qp, kp, u, vcb, lam_v, og, ctx=None):
    b_, n_, _ = qp.shape
    tq = min(256, n_)
    kc = 512
    has_ctx = ctx is not None
    in_specs = [pl.BlockSpec((1, tq, LANES), lambda b, h, i: (b, i, h)),
                pl.BlockSpec((1, n_, LANES), lambda b, h, i: (b, 0, h)),
                pl.BlockSpec((1, n_, LANES), lambda b, h, i: (b, 0, vcb + h))]
    args = [qp, kp, u]
    if has_ctx:
        kpc, uc = ctx
        nc = kpc.shape[1]
        in_specs += [pl.BlockSpec((1, nc, LANES), lambda b, h, i: (b, 0, h)),
                     pl.BlockSpec((1, nc, LANES), lambda b, h, i: (b, 0, vcb + h))]
        args += [kpc, uc]
    in_specs += [pl.BlockSpec((1, LANES), lambda b, h, i: (0, 0)),
                 pl.BlockSpec((1, LANES), lambda b, h, i: (0, 0))]
    args += [lam_v, og]
    scratch = [pltpu.VMEM((n_, 2 * LANES), BF16)]
    if has_ctx:
        scratch.append(pltpu.VMEM((ctx[0].shape[1], 2 * LANES), BF16))
    kern = functools.partial(_attn_kernel, tq=tq, kc=kc, has_ctx=has_ctx)
    return pl.pallas_call(
        kern, grid=(b_, N_DIFF_HEADS, n_ // tq), in_specs=in_specs,
        out_specs=pl.BlockSpec((1, tq, LANES), lambda b, h, i: (b, i, h)),
        out_shape=jax.ShapeDtypeStruct((b_, n_, N_DIFF_HEADS * LANES), BF16),
        scratch_shapes=scratch,
        compiler_params=_cparams(("arbitrary", "arbitrary", "arbitrary")),
        name="diff_attn_ctx" if has_ctx else "diff_attn",
    )(*args)


def _gla_constants(tt, reverse):
    c = GLA_CHUNK
    idx = np.arange(tt)
    same_chunk = (idx[:, None] // c) == (idx[None, :] // c)
    if reverse:
        cum = same_chunk & (idx[None, :] >= idx[:, None])
    else:
        cum = same_chunk & (idx[None, :] <= idx[:, None])
    tot = same_chunk
    sels, masks = [], []
    s = c // 2
    while s >= 1:
        blk = idx // (2 * s)
        right = (idx % (2 * s)) >= s
        same = blk[:, None] == blk[None, :]
        if reverse:
            ref = blk * 2 * s + s
            mask = same & (~right)[:, None] & right[None, :]
        else:
            ref = blk * 2 * s + s - 1
            mask = same & right[:, None] & (~right)[None, :]
        sels.append(idx[None, :] == ref[:, None])
        masks.append(mask)
        s //= 2
    masks.append(idx[:, None] == idx[None, :])
    cum_f = cum.astype(np.float32)
    dall = np.concatenate([cum_f, tot.astype(np.float32)]
                          + [cum_f - sel.astype(np.float32) @ cum_f for sel in sels], axis=0)
    return jnp.asarray(dall).astype(BF16), jnp.asarray(np.stack(masks).astype(np.float32))


def _gla_kernel(q_ref, k_ref, v_ref, lr_ref, w2h_ref, w2l_ref, b_ref, dall_ref, mask_ref, s0_ref,
                o_ref, sfin_ref, s_ref, *, tt, reverse, hk, hv):
    t = pl.program_id(1)

    @pl.when(t == 0)
    def _():
        s_ref[...] = s0_ref[0]

    n_lvl = mask_ref.shape[0] - 1
    lr = lr_ref[0]
    z = (jnp.dot(lr, w2h_ref[...], preferred_element_type=F32)
         + jnp.dot(lr, w2l_ref[...], preferred_element_type=F32) + b_ref[...])
    g = (jnp.minimum(z, 0.0) - jnp.log(1.0 + jnp.exp(-jnp.abs(z)))) * (1.0 / GATE_TAU)
    g_hi = g.astype(BF16)
    g_lo = (g - g_hi.astype(F32)).astype(BF16)
    sums = (jnp.dot(dall_ref[...], g_hi, preferred_element_type=F32)
            + jnp.dot(dall_ref[...], g_lo, preferred_element_type=F32))
    cum = sums[0:tt]
    tot = sums[tt:2 * tt]
    q = q_ref[0].astype(F32) * (hk ** -0.5)
    k = k_ref[0].astype(F32)
    eq = jnp.exp(cum)
    ek = jnp.exp(tot - cum)
    nch = tt // GLA_CHUNK
    order = range(nch - 1, -1, -1) if reverse else range(nch)
    for h in range(N_GLA_HEADS):
        sl = slice(h * hk, (h + 1) * hk)
        qh, kh = q[:, sl], k[:, sl]
        vh = v_ref[0, :, h * hv:(h + 1) * hv]
        a = mask_ref[n_lvl] * lax.dot_general(qh.astype(BF16), kh.astype(BF16), _NT, preferred_element_type=F32)
        for l in range(n_lvl):
            d = sums[(2 + l) * tt:(3 + l) * tt, sl]
            qt = (qh * jnp.exp(jnp.minimum(d, 0.0))).astype(BF16)
            kt = (kh * jnp.exp(jnp.minimum(-d, 0.0))).astype(BF16)
            a = a + mask_ref[l] * lax.dot_general(qt, kt, _NT, preferred_element_type=F32)
        o_h = jnp.dot(a.astype(BF16), vh, preferred_element_type=F32)
        qe = (qh * eq[:, sl]).astype(BF16)
        kd = (kh * ek[:, sl]).astype(BF16)
        outs = [None] * nch
        for c in order:
            rows = slice(c * GLA_CHUNK, (c + 1) * GLA_CHUNK)
            st = s_ref[h]
            outs[c] = o_h[rows] + lax.dot_general(qe[rows], st.astype(BF16), _NT, preferred_element_type=F32)
            dec = jnp.exp(tot[c * GLA_CHUNK:c * GLA_CHUNK + 1, sl])
            s_ref[h] = st * dec + lax.dot_general(vh[rows], kd[rows], _TN, preferred_element_type=F32)
        o_ref[0, :, h * hv:(h + 1) * hv] = jnp.concatenate(outs, axis=0) if nch > 1 else outs[0]

    @pl.when(t == pl.num_programs(1) - 1)
    def _():
        sfin_ref[0] = s_ref[...]


def _gla_dir(u, cbs, w2p, balpha, s0, reverse):
    b_, n_, _ = u.shape
    qcb, kcb, vcb, lrcb, dk, dv = cbs
    hk, hv = dk // N_GLA_HEADS, dv // N_GLA_HEADS
    tt = min(128, n_)
    nt = n_ // tt
    dall, mask = _gla_constants(tt, reverse)
    w2h = w2p.astype(BF16)
    w2l = (w2p - w2h.astype(F32)).astype(BF16)
    tmap = (lambda t: nt - 1 - t) if reverse else (lambda t: t)
    c2 = lambda b, t: (0, 0)
    c3 = lambda b, t: (0, 0, 0)
    kern = functools.partial(_gla_kernel, tt=tt, reverse=reverse, hk=hk, hv=hv)
    return pl.pallas_call(
        kern, grid=(b_, nt),
        in_specs=[pl.BlockSpec((1, tt, dk), lambda b, t: (b, tmap(t), qcb)),
                  pl.BlockSpec((1, tt, dk), lambda b, t: (b, tmap(t), kcb)),
                  pl.BlockSpec((1, tt, dv), lambda b, t: (b, tmap(t), vcb)),
                  pl.BlockSpec((1, tt, LANES), lambda b, t: (b, tmap(t), lrcb)),
                  pl.BlockSpec((LANES, dk), c2),
                  pl.BlockSpec((LANES, dk), c2),
                  pl.BlockSpec((1, dk), c2),
                  pl.BlockSpec(dall.shape, c2),
                  pl.BlockSpec(mask.shape, c3),
                  pl.BlockSpec((1, N_GLA_HEADS, hv, hk), lambda b, t: (b, 0, 0, 0))],
        out_specs=[pl.BlockSpec((1, tt, dv), lambda b, t: (b, tmap(t), 0)),
                   pl.BlockSpec((1, N_GLA_HEADS, hv, hk), lambda b, t: (b, 0, 0, 0))],
        out_shape=[jax.ShapeDtypeStruct((b_, n_, dv), F32),
                   jax.ShapeDtypeStruct((b_, N_GLA_HEADS, hv, hk), F32)],
        scratch_shapes=[pltpu.VMEM((N_GLA_HEADS, hv, hk), F32)],
        compiler_params=_cparams(("arbitrary", "arbitrary")),
        name="gla_bwd" if reverse else "gla_fwd",
    )(u, u, u, u, w2h, w2l, balpha, dall, mask, s0)


def _gla_out_kernel(of_ref, ob_ref, r_ref, gn_ref, o_ref, *, hv):
    o = of_ref[0] + ob_ref[0]
    r = r_ref[0].astype(F32)
    gate = _silu(r)
    gn = gn_ref[...]
    for h in range(o.shape[-1] // hv):
        sl = slice(h * hv, (h + 1) * hv)
        oh = o[:, sl]
        y = oh * lax.rsqrt(jnp.mean(oh * oh, axis=-1, keepdims=True) + EPS) * gn
        o_ref[0, :, sl] = (y * gate[:, sl]).astype(BF16)


def _gla_out(o_f, o_b, u, rcb, gn):
    b_, n_, dv = o_f.shape
    tm = min(512, n_)
    kern = functools.partial(_gla_out_kernel, hv=dv // N_GLA_HEADS)
    blk = lambda cb: pl.BlockSpec((1, tm, dv), lambda b, i: (b, i, cb))
    return pl.pallas_call(
        kern, grid=(b_, n_ // tm),
        in_specs=[blk(0), blk(0), blk(rcb), pl.BlockSpec((1, dv // N_GLA_HEADS), lambda b, i: (0, 0))],
        out_specs=blk(0),
        out_shape=jax.ShapeDtypeStruct((b_, n_, dv), BF16),
        compiler_params=_cparams(("arbitrary", "arbitrary")),
        name="gla_out",
    )(o_f, o_b, u, gn)


HALO = 16


def _conformer_kernel(a_ref, p_ref, n_ref, cw_ref, cb_ref, lg_ref, lb_ref, o_ref, h_ref, rot_ref, *, tm, da):
    i = pl.program_id(1)
    last = pl.num_programs(1) - 1

    def glu(a):
        a = a.astype(F32)
        return a[:, :da] * _sigmoid(a[:, da:])

    h_ref[pl.ds(0, HALO), :] = jnp.where(i > 0, glu(p_ref[0]), 0.0)
    h_ref[pl.ds(HALO, tm), :] = glu(a_ref[0])
    h_ref[pl.ds(HALO + tm, HALO), :] = jnp.where(i < last, glu(n_ref[0]), 0.0)
    pad = (CONV_A_W - 1) // 2
    acc = jnp.zeros((tm, da), F32) + cb_ref[...]
    sub = 8
    span = tm + 2 * HALO - sub
    for r in range(sub):
        if r > 0:
            rot_ref[...] = h_ref[pl.ds(r, span), :]
        src = rot_ref if r > 0 else h_ref
        for a in range(0, 2 * HALO, sub):
            k = a + r - (HALO - pad)
            if 0 <= k < CONV_A_W:
                acc = acc + cw_ref[k:k + 1, :] * src[pl.ds(a, tm), :]
    mu = jnp.mean(acc, axis=-1, keepdims=True)
    xc = acc - mu
    var = jnp.mean(xc * xc, axis=-1, keepdims=True)
    y = xc * lax.rsqrt(var + EPS) * lg_ref[...] + lb_ref[...]
    o_ref[0] = _silu(y).astype(BF16)


def _conformer(u, acb, conv_w, conv_b, ln_g, ln_b):
    b_, n_, _ = u.shape
    da = conv_w.shape[-1]
    tm = min(256, n_)
    nh = tm // HALO
    nblk = n_ // HALO
    kern = functools.partial(_conformer_kernel, tm=tm, da=da)
    row = lambda b, i: (0, 0)
    return pl.pallas_call(
        kern, grid=(b_, n_ // tm),
        in_specs=[pl.BlockSpec((1, tm, 2 * da), lambda b, i: (b, i, acb)),
                  pl.BlockSpec((1, HALO, 2 * da), lambda b, i: (b, jnp.maximum(i * nh - 1, 0), acb)),
                  pl.BlockSpec((1, HALO, 2 * da), lambda b, i: (b, jnp.minimum((i + 1) * nh, nblk - 1), acb)),
                  pl.BlockSpec((CONV_A_W, da), row),
                  pl.BlockSpec((1, da), row), pl.BlockSpec((1, da), row), pl.BlockSpec((1, da), row)],
        out_specs=pl.BlockSpec((1, tm, da), lambda b, i: (b, i, 0)),
        out_shape=jax.ShapeDtypeStruct((b_, n_, da), BF16),
        scratch_shapes=[pltpu.VMEM((tm + 2 * HALO, da), F32), pltpu.VMEM((tm + 2 * HALO - 8, da), F32)],
        compiler_params=_cparams(("arbitrary", "arbitrary")),
        name="conformer",
    )(u, u, u, conv_w, conv_b, ln_g, ln_b)


def _merge_kernel(ha_ref, hb_ref, hc_ref, ga_ref, gb_ref, gc_ref, wa_ref, wb_ref, wc_ref, bg_ref, o_ref):
    acc = None
    for idx, (h_ref, g_ref, w_ref) in enumerate(((ha_ref, ga_ref, wa_ref), (hb_ref, gb_ref, wb_ref),
                                                 (hc_ref, gc_ref, wc_ref))):
        y = jnp.dot(h_ref[0], w_ref[...], preferred_element_type=F32)
        gate = _sigmoid(g_ref[0].astype(F32) + bg_ref[idx])
        acc = gate * y if acc is None else acc + gate * y
    o_ref[0] = acc.astype(BF16)


def _merge(ha, hb, hc, u, wa, wb, wc, bgate, layer):
    b_, n_, kd = ha.shape
    d = wa.shape[-1]
    tm = min(512, n_)
    tn = 512
    nj = d // tn
    hspec = pl.BlockSpec((1, tm, kd), lambda b, i, j: (b, i, 0))
    gspec = lambda br: pl.BlockSpec((1, tm, tn), lambda b, i, j: (b, i, br * nj + j))
    wspec = pl.BlockSpec((None, kd, tn), lambda b, i, j: (layer, 0, j))
    return pl.pallas_call(
        _merge_kernel, grid=(b_, n_ // tm, nj),
        in_specs=[hspec, hspec, hspec, gspec(0), gspec(1), gspec(2), wspec, wspec, wspec,
                  pl.BlockSpec((3, 1, tn), lambda b, i, j: (0, 0, j))],
        out_specs=pl.BlockSpec((1, tm, tn), lambda b, i, j: (b, i, j)),
        out_shape=jax.ShapeDtypeStruct((b_, n_, d), BF16),
        compiler_params=_cparams(("arbitrary", "arbitrary", "arbitrary")),
        name="merge",
    )(ha, hb, hc, u, u, u, wa, wb, wc, bgate)


def _ffn_upact_kernel(x_ref, xp_ref, xn_ref, g_ref, sh_ref, sc_ref, wv_ref, wg_ref, cwv_ref, cwg_ref,
                      cbv_ref, cbg_ref, o_ref, h_ref, *, tm, rc):
    i = pl.program_id(1)
    last_i = pl.num_programs(1) - 1

    @pl.when(pl.program_id(2) == 0)
    def _():
        gain = g_ref[...]
        scale1 = 1.0 + sc_ref[0, 0]
        shift = sh_ref[0, 0]

        def normed(xs):
            ms = jnp.mean(xs * xs, axis=-1, keepdims=True)
            return xs * lax.rsqrt(ms + EPS) * gain * scale1 + shift

        h_ref[pl.ds(0, HALO), :] = jnp.where(i > 0, normed(xp_ref[0]), 0.0).astype(BF16)
        h_ref[pl.ds(HALO + tm, HALO), :] = jnp.where(i < last_i, normed(xn_ref[0]), 0.0).astype(BF16)

        def body(r, carry):
            xs = x_ref[0, pl.ds(pl.multiple_of(r * rc, rc), rc), :]
            h_ref[pl.ds(pl.multiple_of(HALO + r * rc, HALO), rc), :] = normed(xs).astype(BF16)
            return carry

        lax.fori_loop(0, tm // rc, body, 0)

    h = h_ref[...]

    rows = tm + 2 * HALO
    mid = slice(HALO, HALO + tm)

    def conv(w_ref, cw_ref, cb_ref):
        u = jnp.dot(h, w_ref[...], preferred_element_type=F32)
        prev = pltpu.roll(u, 1, 0)[mid]
        nxt = pltpu.roll(u, rows - 1, 0)[mid]
        return cw_ref[0:1, :] * prev + cw_ref[1:2, :] * u[mid] + cw_ref[2:3, :] * nxt + cb_ref[...]

    val = conv(wv_ref, cwv_ref, cbv_ref)
    gt = conv(wg_ref, cwg_ref, cbg_ref)
    o_ref[0] = (_silu(gt) * val).astype(BF16)


def _ffn_upact(x, wup, layer, conv_w, conv_b, gain, mod4, shi, sci, row0, rstr):
    b_, n_, d = x.shape
    dff = wup.shape[-1] // 2
    tm = min(512, n_)
    tf = _pick_tile(dff, 1408)
    nf = dff // tf
    nh = tm // HALO
    nblk = n_ // HALO
    kern = functools.partial(_ffn_upact_kernel, tm=tm, rc=min(128, tm))
    modspec = lambda which: pl.BlockSpec((1, 1, 1, d), lambda b, i, j: (row0 + b * rstr, which, 0, 0))
    wspec = lambda off: pl.BlockSpec((None, d, tf), lambda b, i, j: (layer, 0, off + j))
    cw = lambda off: pl.BlockSpec((3, tf), lambda b, i, j: (0, off + j))
    cb = lambda off: pl.BlockSpec((1, tf), lambda b, i, j: (0, off + j))
    return pl.pallas_call(
        kern, grid=(b_, n_ // tm, nf),
        in_specs=[pl.BlockSpec((1, tm, d), lambda b, i, j: (b, i, 0)),
                  pl.BlockSpec((1, HALO, d), lambda b, i, j: (b, jnp.maximum(i * nh - 1, 0), 0)),
                  pl.BlockSpec((1, HALO, d), lambda b, i, j: (b, jnp.minimum((i + 1) * nh, nblk - 1), 0)),
                  pl.BlockSpec((1, d), lambda b, i, j: (0, 0)),
                  modspec(shi), modspec(sci), wspec(0), wspec(nf), cw(0), cw(nf), cb(0), cb(nf)],
        out_specs=pl.BlockSpec((1, tm, tf), lambda b, i, j: (b, i, j)),
        out_shape=jax.ShapeDtypeStruct((b_, n_, dff), BF16),
        scratch_shapes=[pltpu.VMEM((tm + 2 * HALO, d), BF16)],
        compiler_params=_cparams(("arbitrary", "arbitrary", "arbitrary")),
        name="ffn_upact",
    )(x, x, x, gain, mod4, mod4, wup, wup, conv_w, conv_w, conv_b, conv_b)


def _rope_tables(n_tokens):
    rows = n_tokens // GRID_W
    row = jnp.broadcast_to(jnp.arange(rows, dtype=F32)[:, None], (rows, GRID_W)).reshape(-1)
    col = jnp.broadcast_to(jnp.arange(GRID_W, dtype=F32)[None, :], (rows, GRID_W)).reshape(-1)
    n_freq = DIFF_QK_DIM // 4
    inv = ROPE_BASE ** (-jnp.arange(n_freq, dtype=F32) / n_freq)
    ang = jnp.concatenate([row[:, None] * inv, col[:, None] * inv], axis=-1)
    cos, sin = jnp.cos(ang), jnp.sin(ang)
    cos_t = jnp.tile(cos, (1, LANES // cos.shape[1]))
    sin_t = jnp.tile(jnp.concatenate([-sin, sin], axis=-1), (1, LANES // (2 * sin.shape[1])))
    return cos_t, sin_t


def kernel(x, c, ctx, c_ctx, w_ada, b_ada, g_norm1, w_in, b_gate, conv_a_w, conv_a_b, ln_a_g, ln_a_b, w_a_out,
           qn_g, kn_g, lam_q1, lam_k1, lam_q2, lam_k2, subln_g, w_b_out, w_alpha2, b_alpha, gn_c_g, w_c_out,
           w_o, g_norm2, w_up, conv_f_w, conv_f_b, w_down):
    b_, n_lat, d = x.shape
    n_ctx = ctx.shape[1]
    depth = w_ada.shape[0]
    da = conv_a_w.shape[-1]
    qk_w = 2 * N_DIFF_HEADS * DIFF_QK_DIM
    db = w_b_out.shape[1]
    dc = w_c_out.shape[1]
    dk = w_alpha2.shape[-1]
    dff = w_down.shape[1]

    sizes = (2 * da, qk_w, qk_w, db, dk, dk, dc, dc, 2 * GATE_RANK, 3 * d)
    offs = np.concatenate([[0], np.cumsum(sizes)])
    o_a, o_qd, o_kd, o_vd, o_qg, o_kg, o_vg, o_r, o_lr, o_gate = (int(v) for v in offs[:-1])
    new_order = [(o_gate, 3 * d), (o_a, 2 * da), (o_qd, qk_w), (o_kd, qk_w), (o_vd, db), (o_vg, dc), (o_r, dc),
                 (o_qg, dk), (o_kg, dk), (o_lr, 2 * GATE_RANK)]
    starts = np.concatenate([[0], np.cumsum([s for _, s in new_order])])
    p_gate, p_a, p_qd, p_kd, p_vd, p_vg, p_r, p_qg, p_kg, p_lr = (int(v) for v in starts[:-1])
    tn_in = 768
    np_cols = -(-(p_lr + LANES) // tn_in) * tn_in
    assert p_a % (2 * da) == 0 and p_qd % qk_w == 0 and p_kd % qk_w == 0 and p_vd % LANES == 0
    assert p_vg % dc == 0 and p_r % dc == 0 and p_qg % dk == 0 and p_kg % dk == 0 and p_lr % LANES == 0

    def relayout_w_in(w):
        parts = [w[:, o:o + s] for o, s in new_order]
        parts.append(jnp.zeros((w.shape[0], np_cols - (p_lr + 2 * GATE_RANK)), w.dtype))
        return jnp.concatenate(parts, axis=1).astype(BF16)

    cos_l, sin_l = _rope_tables(n_lat)
    cos_c = jnp.ones((n_ctx, LANES), F32)
    sin_c = jnp.zeros((n_ctx, LANES), F32)
    grp = np.arange(LANES) // DIFF_QK_DIM
    gmat = jnp.asarray((grp[:, None] == grp[None, :]).astype(np.float32)).astype(BF16)

    cvecs = jnp.zeros((8, d), F32).at[:b_].set(c).at[b_].set(c_ctx)
    ctx_row = b_

    s_zero = jnp.zeros((b_, N_GLA_HEADS, dc // N_GLA_HEADS, dk // N_GLA_HEADS), F32)
    wa, wb, wc = w_a_out.astype(BF16), w_b_out.astype(BF16), w_c_out.astype(BF16)
    wo, wup, wdn = w_o.astype(BF16), w_up.astype(BF16), w_down.astype(BF16)

    for l in range(depth):
        last = l == depth - 1
        lam_init = 0.8 - 0.6 * math.exp(-0.3 * l)
        lam = (jnp.exp(jnp.sum(lam_q1[l] * lam_k1[l])) - jnp.exp(jnp.sum(lam_q2[l] * lam_k2[l])) + lam_init)
        lam_v = jnp.full((1, LANES), lam, F32)
        og = (subln_g[l] * (1.0 - lam_init)).reshape(1, LANES)
        gq = (jnp.tile(qn_g[l], LANES // DIFF_QK_DIM) * (DIFF_QK_DIM ** -0.5 * math.log2(math.e))).reshape(1, LANES)
        gk = jnp.tile(kn_g[l], LANES // DIFF_QK_DIM).reshape(1, LANES)

        mod = _adaln(cvecs, w_ada, b_ada, l)
        mod4 = mod.reshape(8, 6, 1, d)

        w_in_p = relayout_w_in(w_in[l])
        g1 =g_norm1[l].reshape(1, d)
        g2 = g_norm2[l].reshape(1, d)
        bg = b_gate[l].reshape(3, 1, d)
        w2 = w_alpha2[l]
        w2p = [jnp.zeros((LANES, dk), F32).at[z * GATE_RANK:(z + 1) * GATE_RANK].set(w2[z]) for z in range(2)]
        bal = [b_alpha[l, z].reshape(1, dk) for z in range(2)]
        gla_cbs = (p_qg // dk, p_kg // dk, p_vg // dc, p_lr // LANES, dk, dc)

        u_l = _mm(x, w_in_p, tn=768, tm=1024, out_dtype=BF16, norm=(g1, mod4, 0, 1, 0, 1), name="in_proj")
        ctx_flat = ctx.reshape(1, b_ * n_ctx, d)
        u_c = _mm(ctx_flat, w_in_p, tn=2432, out_dtype=BF16, norm=(g1, mod4, 0, 1, ctx_row, 0),
                  name="in_proj_ctx").reshape(b_, n_ctx, np_cols)

        qp_l, kp_l = _qkprep(u_l, cos_l, sin_l, gq, gk, gmat, p_qd // qk_w, p_kd // qk_w)
        qp_c, kp_c = _qkprep(u_c, cos_c, sin_c, gq, gk, gmat, p_qd // qk_w, p_kd // qk_w)
        hb_l = _attn(qp_l, kp_l, u_l, p_vd // LANES, lam_v, og, ctx=(kp_c, u_c))

        of_c, sf = _gla_dir(u_c, gla_cbs, w2p[0], bal[0], s_zero, False)
        ob_c, sb = _gla_dir(u_c, gla_cbs, w2p[1], bal[1], s_zero, True)
        of_l, _ = _gla_dir(u_l, gla_cbs, w2p[0], bal[0], sf, False)
        ob_l, _ = _gla_dir(u_l, gla_cbs, w2p[1], bal[1], sb, True)
        gn = gn_c_g[l].reshape(1, -1)
        hc_l = _gla_out(of_l, ob_l, u_l, p_r // dc, gn)

        cb_a, lg_a, lb_a = conv_a_b[l].reshape(1, da), ln_a_g[l].reshape(1, da), ln_a_b[l].reshape(1, da)
        ha_l = _conformer(u_l, p_a // (2 * da), conv_a_w[l], cb_a, lg_a, lb_a)

        m_l = _merge(ha_l, hb_l, hc_l, u_l, wa, wb, wc, bg, l)
        x_mid = _mm(m_l, wo, tn=2048, out_dtype=F32, residual=(x, mod4, 2, 0, 1), layer=l, name="out_proj")

        cfb = conv_f_b[l].reshape(1, 2 * dff)
        act_l = _ffn_upact(x_mid, wup, l, conv_f_w[l], cfb, g2, mod4, 3, 4, 0, 1)
        x_new = _mm(act_l, wdn, tn=512, tm=1024, out_dtype=F32, residual=(x_mid, mod4, 5, 0, 1), layer=l,
                    name="ffn_down")

        if not last:
            hb_c = _attn(qp_c, kp_c, u_c, p_vd // LANES, lam_v, og)
            hc_c = _gla_out(of_c, ob_c, u_c, p_r // dc, gn)
            ha_c = _conformer(u_c, p_a // (2 * da), conv_a_w[l], cb_a, lg_a, lb_a)
            m_c = _merge(ha_c, hb_c, hc_c, u_c, wa, wb, wc, bg, l)
            ctx_mid = _mm(m_c.reshape(1, b_ * n_ctx, d), wo, tn=2048, out_dtype=F32,
                          residual=(ctx_flat, mod4, 2, ctx_row, 0), layer=l, name="out_proj_ctx")
            act_c = _ffn_upact(ctx_mid.reshape(b_, n_ctx, d), wup, l, conv_f_w[l], cfb, g2, mod4, 3, 4, ctx_row, 0)
            ctx = _mm(act_c.reshape(1, b_ * n_ctx, dff), wdn, tn=1024, out_dtype=F32,
                      residual=(ctx_mid, mod4, 5, ctx_row, 0), layer=l, name="ffn_down_ctx").reshape(b_, n_ctx, d)

        x = x_new

    return x
```

```python
import functools
import math

import numpy as np
import jax
import jax.numpy as jnp
from jax import lax
from jax.experimental import pallas as pl
from jax.experimental.pallas import tpu as pltpu

F32 = jnp.float32
BF16 = jnp.bfloat16
HIGHEST = lax.Precision.HIGHEST

EPS = 1e-6
GRID_W = 64
ROPE_BASE = 10000.0
N_DIFF_HEADS = 8
DIFF_QK_DIM = 64
N_GLA_HEADS = 4
GATE_RANK = 16
GATE_TAU = 16.0
GLA_CHUNK = 64
CONV_A_W = 31

LANES = 128
VMEM_LIMIT_BYTES = 56 * 1024 * 1024

_NT = (((1,), (1,)), ((), ()))
_TN = (((0,), (0,)), ((), ()))


def _cparams(sem):
    return pltpu.CompilerParams(dimension_semantics=sem, vmem_limit_bytes=VMEM_LIMIT_BYTES)


def _pick_tile(total, target):
    best = LANES
    for t in range(LANES, min(total, target) + 1, LANES):
        if total % t == 0:
            best = t
    return best


def _sigmoid(x):
    return 1.0 / (1.0 + jnp.exp(-x))


def _silu(x):
    return x * _sigmoid(x)


def _adaln_kernel(c_ref, w_ref, b_ref, o_ref):
    a = _silu(c_ref[...])
    o_ref[...] = jnp.dot(a, w_ref[...], preferred_element_type=F32, precision=HIGHEST) + b_ref[...]


def _adaln(cvecs, w_ada, b_ada, layer):
    rows, d = cvecs.shape
    n = w_ada.shape[-1]
    tn = 1024
    return pl.pallas_call(
        _adaln_kernel,
        grid=(n // tn,),
        in_specs=[pl.BlockSpec((rows, d), lambda j: (0, 0)),
                  pl.BlockSpec((None, d, tn), lambda j: (layer, 0, j)),
                  pl.BlockSpec((None, 1, tn), lambda j: (layer, 0, j))],
        out_specs=pl.BlockSpec((rows, tn), lambda j: (0, j)),
        out_shape=jax.ShapeDtypeStruct((rows, n), F32),
        compiler_params=_cparams(("arbitrary",)),
        name="adaln",
    )(cvecs, w_ada, b_ada.reshape(b_ada.shape[0], 1, n))


def _mm_kernel(*refs, norm, residual, tm, rc):
    it = iter(refs)
    x_ref = next(it)
    if norm:
        g_ref, sh_ref, sc_ref = next(it), next(it), next(it)
    w_ref = next(it)
    if residual:
        res_ref, gate_ref = next(it), next(it)
    o_ref = next(it)
    if norm:
        h_ref = next(it)

        @pl.when(pl.program_id(2) == 0)
        def _():
            gain = g_ref[...]
            scale1 = 1.0 + sc_ref[0, 0]
            shift = sh_ref[0, 0]

            def body(r, carry):
                xs = x_ref[0, pl.ds(pl.multiple_of(r * rc, rc), rc), :]
                ms = jnp.mean(xs * xs, axis=-1, keepdims=True)
                y = xs * lax.rsqrt(ms + EPS) * gain
                h_ref[pl.ds(pl.multiple_of(r * rc, rc), rc), :] = (y * scale1 + shift).astype(BF16)
                return carry

            lax.fori_loop(0, tm // rc, body, 0)

        lhs = h_ref[...]
    else:
        lhs = x_ref[0]
    acc = jnp.dot(lhs, w_ref[...], preferred_element_type=F32)
    if residual:
        o_ref[0] = res_ref[0] + gate_ref[0, 0] * acc
    else:
        o_ref[0] = acc.astype(o_ref.dtype)


def _mm(x, w, *, tn, out_dtype, norm=None, residual=None, layer=None, tm=512, name):
    g_, r_, k_ = x.shape
    n_ = w.shape[-1]
    tm = min(tm, r_)
    tn = _pick_tile(n_, tn)
    assert r_ % tm == 0
    grid = (g_, r_ // tm, n_ // tn)
    in_specs = [pl.BlockSpec((1, tm, k_), lambda g, i, j: (g, i, 0))]
    args = [x]
    scratch = []
    if norm is not None:
        gain, mod4, shi, sci, row0, rstr = norm
        in_specs += [pl.BlockSpec((1, k_), lambda g, i, j: (0, 0)),
                     pl.BlockSpec((1, 1, 1, k_), lambda g, i, j: (row0 + g * rstr, shi, 0, 0)),
                     pl.BlockSpec((1, 1, 1, k_), lambda g, i, j: (row0 + g * rstr, sci, 0, 0))]
        args += [gain, mod4, mod4]
        scratch = [pltpu.VMEM((tm, k_), BF16)]
    if layer is None:
        in_specs.append(pl.BlockSpec((k_, tn), lambda g, i, j: (0, j)))
    else:
        in_specs.append(pl.BlockSpec((None, k_, tn), lambda g, i, j: (layer, 0, j)))
    args.append(w)
    if residual is not None:
        res, mod4r, gti, rrow0, rrstr = residual
        in_specs += [pl.BlockSpec((1, tm, tn), lambda g, i, j: (g, i, j)),
                     pl.BlockSpec((1, 1, 1, tn), lambda g, i, j: (rrow0 + g * rrstr, gti, 0, j))]
        args += [res, mod4r]
    kern = functools.partial(_mm_kernel, norm=norm is not None, residual=residual is not None,
                             tm=tm, rc=min(128, tm))
    return pl.pallas_call(
        kern, grid=grid, in_specs=in_specs,
        out_specs=pl.BlockSpec((1, tm, tn), lambda g, i, j: (g, i, j)),
        out_shape=jax.ShapeDtypeStruct((g_, r_, n_), out_dtype),
        scratch_shapes=scratch,
        compiler_params=_cparams(("arbitrary", "arbitrary", "arbitrary")),
        name=name,
    )(*args)


def _qkprep_kernel(q_ref, k_ref, cos_ref, sin_ref, gq_ref, gk_ref, gm_ref, qo_ref, ko_ref, *, tm):
    cos = cos_ref[...]
    sin = sin_ref[...]
    gm = gm_ref[...]
    lane = lax.broadcasted_iota(jnp.int32, (tm, LANES), 1)
    first = (lane % DIFF_QK_DIM) < (DIFF_QK_DIM // 2)
    for src, g_ref, dst in ((q_ref, gq_ref, qo_ref), (k_ref, gk_ref, ko_ref)):
        gain = g_ref[...]
        for h in range(src.shape[-1] // LANES):
            sl = slice(h * LANES, (h + 1) * LANES)
            x = src[0, :, sl].astype(F32)
            sq = x * x
            hi = sq.astype(BF16)
            lo = (sq - hi.astype(F32)).astype(BF16)
            ss = (jnp.dot(hi, gm, preferred_element_type=F32) + jnp.dot(lo, gm, preferred_element_type=F32))
            y = x * lax.rsqrt(ss * (1.0 / DIFF_QK_DIM) + EPS) * gain
            partner = jnp.where(first, pltpu.roll(y, LANES - DIFF_QK_DIM // 2, 1),
                                pltpu.roll(y, DIFF_QK_DIM // 2, 1))
            dst[0, :, sl] = (y * cos + partner * sin).astype(BF16)


def _qkprep(u, cos_t, sin_t, gq, gk, gmat, qcb, kcb):
    b_, n_, _ = u.shape
    width = 2 * N_DIFF_HEADS * DIFF_QK_DIM
    tm = min(512, n_)
    kern = functools.partial(_qkprep_kernel, tm=tm)
    out = jax.ShapeDtypeStruct((b_, n_, width), BF16)
    return pl.pallas_call(
        kern, grid=(b_, n_ // tm),
        in_specs=[pl.BlockSpec((1, tm, width), lambda b, i: (b, i, qcb)),
                  pl.BlockSpec((1, tm, width), lambda b, i: (b, i, kcb)),
                  pl.BlockSpec((tm, LANES), lambda b, i: (i, 0)),
                  pl.BlockSpec((tm, LANES), lambda b, i: (i, 0)),
                  pl.BlockSpec((1, LANES), lambda b, i: (0, 0)),
                  pl.BlockSpec((1, LANES), lambda b, i: (0, 0)),
                  pl.BlockSpec((LANES, LANES), lambda b, i: (0, 0))],
        out_specs=[pl.BlockSpec((1, tm, width), lambda b, i: (b, i, 0)),
                   pl.BlockSpec((1, tm, width), lambda b, i: (b, i, 0))],
        out_shape=[out, out],
        compiler_params=_cparams(("arbitrary", "arbitrary")),
        name="qkprep",
    )(u, u, cos_t, sin_t, gq, gk, gmat)


def _attn_kernel(*refs, tq, kc, has_ctx):
    if has_ctx:
        q_ref, kl_ref, vl_ref, kc_ref, vc_ref, lam_ref, og_ref, o_ref, vxl_ref, vxc_ref = refs
    else:
        q_ref, kl_ref, vl_ref, lam_ref, og_ref, o_ref, vxl_ref = refs

    @pl.when(pl.program_id(2) == 0)
    def _():
        for v_ref, vx_ref in ((vl_ref, vxl_ref),) + (((vc_ref, vxc_ref),) if has_ctx else ()):
            vx_ref[:, :LANES] = v_ref[0]
            vx_ref[:, LANES:] = jnp.ones((vx_ref.shape[0], LANES), BF16)

    q = q_ref[0]
    lane = lax.broadcasted_iota(jnp.int32, (tq, LANES), 1)
    zero = jnp.zeros_like(q)
    qq = jnp.concatenate([jnp.where(lane < DIFF_QK_DIM, q, zero),
                          jnp.where(lane >= DIFF_QK_DIM, q, zero)], axis=0)
    chunks = [(kl_ref, vxl_ref, c * kc, min(kc, kl_ref.shape[1] - c * kc))
              for c in range(-(-kl_ref.shape[1] // kc))]
    if has_ctx:
        chunks += [(kc_ref, vxc_ref, c * kc, min(kc, kc_ref.shape[1] - c * kc))
                   for c in range(-(-kc_ref.shape[1] // kc))]
    m = None
    acc = None
    for k_ref, vx_ref, start, size in chunks:
        s = lax.dot_general(qq, k_ref[0, pl.ds(start, size), :], _NT, preferred_element_type=F32).astype(BF16)
        cmax = jnp.max(s, axis=-1, keepdims=True).astype(F32)
        m_new = cmax if m is None else jnp.maximum(m, cmax)
        p = jnp.exp2(s - m_new.astype(BF16))
        pv = jnp.dot(p, vx_ref[pl.ds(start, size), :], preferred_element_type=F32)
        acc = pv if acc is None else acc * jnp.exp2(m - m_new) + pv
        m = m_new
    lam = lam_ref[0:1, 0:1]
    o1 = acc[:tq, :LANES] / acc[:tq, LANES:LANES + 1]
    o2 = acc[tq:, :LANES] / acc[tq:, LANES:LANES + 1]
    o = o1 - lam * o2
    o = o * lax.rsqrt(jnp.mean(o * o, axis=-1, keepdims=True) + EPS) * og_ref[...]
    o_ref[0] = o.astype(BF16)


def _attn(qp, kp, u, vcb, lam_v, og, ctx=None):
    b_, n_, _ = qp.shape
    tq = min(512, n_)
    kc = 512
    has_ctx = ctx is not None
    in_specs = [pl.BlockSpec((1, tq, LANES), lambda b, h, i: (b, i, h)),
                pl.BlockSpec((1, n_, LANES), lambda b, h, i: (b, 0, h)),
                pl.BlockSpec((1, n_, LANES), lambda b, h, i: (b, 0, vcb + h))]
    args = [qp, kp, u]
    if has_ctx:
        kpc, uc = ctx
        nc = kpc.shape[1]
        in_specs += [pl.BlockSpec((1, nc, LANES), lambda b, h, i: (b, 0, h)),
                     pl.BlockSpec((1, nc, LANES), lambda b, h, i: (b, 0, vcb + h))]
        args += [kpc, uc]
    in_specs += [pl.BlockSpec((1, LANES), lambda b, h, i: (0, 0)),
                 pl.BlockSpec((1, LANES), lambda b, h, i: (0, 0))]
    args += [lam_v, og]
    scratch = [pltpu.VMEM((n_, 2 * LANES), BF16)]
    if has_ctx:
        scratch.append(pltpu.VMEM((ctx[0].shape[1], 2 * LANES), BF16))
    kern = functools.partial(_attn_kernel, tq=tq, kc=kc, has_ctx=has_ctx)
    return pl.pallas_call(
        kern, grid=(b_, N_DIFF_HEADS, n_ // tq), in_specs=in_specs,
        out_specs=pl.BlockSpec((1, tq, LANES), lambda b, h, i: (b, i, h)),
        out_shape=jax.ShapeDtypeStruct((b_, n_, N_DIFF_HEADS * LANES), BF16),
        scratch_shapes=scratch,
        compiler_params=_cparams(("arbitrary", "arbitrary", "arbitrary")),
        name="diff_attn_ctx" if has_ctx else "diff_attn",
    )(*args)


def _gla_constants(tt, reverse):
    c = GLA_CHUNK
    idx = np.arange(tt)
    same_chunk = (idx[:, None] // c) == (idx[None, :] // c)
    if reverse:
        cum = same_chunk & (idx[None, :] >= idx[:, None])
    else:
        cum = same_chunk & (idx[None, :] <= idx[:, None])
    tot = same_chunk
    sels, masks = [], []
    s = c // 2
    while s >= 1:
        blk = idx // (2 * s)
        right = (idx % (2 * s)) >= s
        same = blk[:, None] == blk[None, :]
        if reverse:
            ref = blk * 2 * s + s
            mask = same & (~right)[:, None] & right[None, :]
        else:
            ref = blk * 2 * s + s - 1
            mask = same & right[:, None] & (~right)[None, :]
        sels.append(idx[None, :] == ref[:, None])
        masks.append(mask)
        s //= 2
    masks.append(idx[:, None] == idx[None, :])
    cum_f = cum.astype(np.float32)
    dall = np.concatenate([cum_f, tot.astype(np.float32)]
                          + [cum_f - sel.astype(np.float32) @ cum_f for sel in sels], axis=0)
    return jnp.asarray(dall).astype(BF16), jnp.asarray(np.stack(masks).astype(np.float32))


def _gla_kernel(q_ref, k_ref, v_ref, lr_ref, w2h_ref, w2l_ref, b_ref, dall_ref, mask_ref, s0_ref,
                o_ref, sfin_ref, s_ref, *, tt, reverse, hk, hv):
    t = pl.program_id(1)

    @pl.when(t == 0)
    def _():
        s_ref[...] = s0_ref[0]

    n_lvl = mask_ref.shape[0] - 1
    lr = lr_ref[0]
    z = (jnp.dot(lr, w2h_ref[...], preferred_element_type=F32)
         + jnp.dot(lr, w2l_ref[...], preferred_element_type=F32) + b_ref[...])
    g = (jnp.minimum(z, 0.0) - jnp.log(1.0 + jnp.exp(-jnp.abs(z)))) * (1.0 / GATE_TAU)
    g_hi = g.astype(BF16)
    g_lo = (g - g_hi.astype(F32)).astype(BF16)
    sums = (jnp.dot(dall_ref[...], g_hi, preferred_element_type=F32)
            + jnp.dot(dall_ref[...], g_lo, preferred_element_type=F32))
    cum = sums[0:tt]
    tot = sums[tt:2 * tt]
    q = q_ref[0].astype(F32) * (hk ** -0.5)
    k = k_ref[0].astype(F32)
    eq = jnp.exp(cum)
    ek = jnp.exp(tot - cum)
    nch = tt // GLA_CHUNK
    order = range(nch - 1, -1, -1) if reverse else range(nch)
    for h in range(N_GLA_HEADS):
        sl = slice(h * hk, (h + 1) * hk)
        qh, kh = q[:, sl], k[:, sl]
        vh = v_ref[0, :, h * hv:(h + 1) * hv]
        a = mask_ref[n_lvl] * lax.dot_general(qh.astype(BF16), kh.astype(BF16), _NT, preferred_element_type=F32)
        for l in range(n_lvl):
            d = sums[(2 + l) * tt:(3 + l) * tt, sl]
            qt = (qh * jnp.exp(jnp.minimum(d, 0.0))).astype(BF16)
            kt = (kh * jnp.exp(jnp.minimum(-d, 0.0))).astype(BF16)
            a = a + mask_ref[l] * lax.dot_general(qt, kt, _NT, preferred_element_type=F32)
        o_h = jnp.dot(a.astype(BF16), vh, preferred_element_type=F32)
        qe = (qh * eq[:, sl]).astype(BF16)
        kd = (kh * ek[:, sl]).astype(BF16)
        outs = [None] * nch
        for c in order:
            rows = slice(c * GLA_CHUNK, (c + 1) * GLA_CHUNK)
            st = s_ref[h]
            outs[c] = o_h[rows] + lax.dot_general(qe[rows], st.astype(BF16), _NT, preferred_element_type=F32)
            dec = jnp.exp(tot[c * GLA_CHUNK:c * GLA_CHUNK + 1, sl])
            s_ref[h] = st * dec + lax.dot_general(vh[rows], kd[rows], _TN, preferred_element_type=F32)
        o_ref[0, :, h * hv:(h + 1) * hv] = jnp.concatenate(outs, axis=0) if nch > 1 else outs[0]

    @pl.when(t == pl.num_programs(1) - 1)
    def _():
        sfin_ref[0] = s_ref[...]


def _gla_dir(u, cbs, w2p, balpha, s0, reverse):
    b_, n_, _ = u.shape
    qcb, kcb, vcb, lrcb, dk, dv = cbs
    hk, hv = dk // N_GLA_HEADS, dv // N_GLA_HEADS
    tt = min(256, n_)
    nt = n_ // tt
    dall, mask = _gla_constants(tt, reverse)
    w2h = w2p.astype(BF16)
    w2l = (w2p - w2h.astype(F32)).astype(BF16)
    tmap = (lambda t: nt - 1 - t) if reverse else (lambda t: t)
    c2 = lambda b, t: (0, 0)
    c3 = lambda b, t: (0, 0, 0)
    kern = functools.partial(_gla_kernel, tt=tt, reverse=reverse, hk=hk, hv=hv)
    return pl.pallas_call(
        kern, grid=(b_, nt),
        in_specs=[pl.BlockSpec((1, tt, dk), lambda b, t: (b, tmap(t), qcb)),
                  pl.BlockSpec((1, tt, dk), lambda b, t: (b, tmap(t), kcb)),
                  pl.BlockSpec((1, tt, dv), lambda b, t: (b, tmap(t), vcb)),
                  pl.BlockSpec((1, tt, LANES), lambda b, t: (b, tmap(t), lrcb)),
                  pl.BlockSpec((LANES, dk), c2),
                  pl.BlockSpec((LANES, dk), c2),
                  pl.BlockSpec((1, dk), c2),
                  pl.BlockSpec(dall.shape, c2),
                  pl.BlockSpec(mask.shape, c3),
                  pl.BlockSpec((1, N_GLA_HEADS, hv, hk), lambda b, t: (b, 0, 0, 0))],
        out_specs=[pl.BlockSpec((1, tt, dv), lambda b, t: (b, tmap(t), 0)),
                   pl.BlockSpec((1, N_GLA_HEADS, hv, hk), lambda b, t: (b, 0, 0, 0))],
        out_shape=[jax.ShapeDtypeStruct((b_, n_, dv), F32),
                   jax.ShapeDtypeStruct((b_, N_GLA_HEADS, hv, hk), F32)],
        scratch_shapes=[pltpu.VMEM((N_GLA_HEADS, hv, hk), F32)],
        compiler_params=_cparams(("arbitrary", "arbitrary")),
        name="gla_bwd" if reverse else "gla_fwd",
    )(u, u, u, u, w2h, w2l, balpha, dall, mask, s0)


def _gla_out_kernel(of_ref, ob_ref, r_ref, gn_ref, o_ref, *, hv):
    o = of_ref[0] + ob_ref[0]
    r = r_ref[0].astype(F32)
    gate = _silu(r)
    gn = gn_ref[...]
    for h in range(o.shape[-1] // hv):
        sl = slice(h * hv, (h + 1) * hv)
        oh = o[:, sl]
        y = oh * lax.rsqrt(jnp.mean(oh * oh, axis=-1, keepdims=True) + EPS) * gn
        o_ref[0, :, sl] = (y * gate[:, sl]).astype(BF16)


def _gla_out(o_f, o_b, u, rcb, gn):
    b_, n_, dv = o_f.shape
    tm = min(512, n_)
    kern = functools.partial(_gla_out_kernel, hv=dv // N_GLA_HEADS)
    blk = lambda cb: pl.BlockSpec((1, tm, dv), lambda b, i: (b, i, cb))
    return pl.pallas_call(
        kern, grid=(b_, n_ // tm),
        in_specs=[blk(0), blk(0), blk(rcb), pl.BlockSpec((1, dv // N_GLA_HEADS), lambda b, i: (0, 0))],
        out_specs=blk(0),
        out_shape=jax.ShapeDtypeStruct((b_, n_, dv), BF16),
        compiler_params=_cparams(("arbitrary", "arbitrary")),
        name="gla_out",
    )(o_f, o_b, u, gn)


HALO = 16


def _conformer_kernel(a_ref, p_ref, n_ref, cw_ref, cb_ref, lg_ref, lb_ref, o_ref, h_ref, rot_ref, *, tm, da):
    i = pl.program_id(1)
    last = pl.num_programs(1) - 1

    def glu(a):
        a = a.astype(F32)
        return a[:, :da] * _sigmoid(a[:, da:])

    h_ref[pl.ds(0, HALO), :] = jnp.where(i > 0, glu(p_ref[0]), 0.0)
    h_ref[pl.ds(HALO, tm), :] = glu(a_ref[0])
    h_ref[pl.ds(HALO + tm, HALO), :] = jnp.where(i < last, glu(n_ref[0]), 0.0)
    pad = (CONV_A_W - 1) // 2
    acc = jnp.zeros((tm, da), F32) + cb_ref[...]
    sub = 8
    span = tm + 2 * HALO - sub
    for r in range(sub):
        if r > 0:
            rot_ref[...] = h_ref[pl.ds(r, span), :]
        src = rot_ref if r > 0 else h_ref
        for a in range(0, 2 * HALO, sub):
            k = a + r - (HALO - pad)
            if 0 <= k < CONV_A_W:
                acc = acc + cw_ref[k:k + 1, :] * src[pl.ds(a, tm), :]
    mu = jnp.mean(acc, axis=-1, keepdims=True)
    xc = acc - mu
    var = jnp.mean(xc * xc, axis=-1, keepdims=True)
    y = xc * lax.rsqrt(var + EPS) * lg_ref[...] + lb_ref[...]
    o_ref[0] = _silu(y).astype(BF16)


def _conformer(u, acb, conv_w, conv_b, ln_g, ln_b):
    b_, n_, _ = u.shape
    da = conv_w.shape[-1]
    tm = min(256, n_)
    nh = tm // HALO
    nblk = n_ // HALO
    kern = functools.partial(_conformer_kernel, tm=tm, da=da)
    row = lambda b, i: (0, 0)
    return pl.pallas_call(
        kern, grid=(b_, n_ // tm),
        in_specs=[pl.BlockSpec((1, tm, 2 * da), lambda b, i: (b, i, acb)),
                  pl.BlockSpec((1, HALO, 2 * da), lambda b, i: (b, jnp.maximum(i * nh - 1, 0), acb)),
                  pl.BlockSpec((1, HALO, 2 * da), lambda b, i: (b, jnp.minimum((i + 1) * nh, nblk - 1), acb)),
                  pl.BlockSpec((CONV_A_W, da), row),
                  pl.BlockSpec((1, da), row), pl.BlockSpec((1, da), row), pl.BlockSpec((1, da), row)],
        out_specs=pl.BlockSpec((1, tm, da), lambda b, i: (b, i, 0)),
        out_shape=jax.ShapeDtypeStruct((b_, n_, da), BF16),
        scratch_shapes=[pltpu.VMEM((tm + 2 * HALO, da), F32), pltpu.VMEM((tm + 2 * HALO - 8, da), F32)],
        compiler_params=_cparams(("arbitrary", "arbitrary")),
        name="conformer",
    )(u, u, u, conv_w, conv_b, ln_g, ln_b)


def _merge_kernel(ha_ref, hb_ref, hc_ref, ga_ref, gb_ref, gc_ref, wa_ref, wb_ref, wc_ref, bg_ref, o_ref):
    acc = None
    for idx, (h_ref, g_ref, w_ref) in enumerate(((ha_ref, ga_ref, wa_ref), (hb_ref, gb_ref, wb_ref),
                                                 (hc_ref, gc_ref, wc_ref))):
        y = jnp.dot(h_ref[0], w_ref[...], preferred_element_type=F32)
        gate = _sigmoid(g_ref[0].astype(F32) + bg_ref[idx])
        acc = gate * y if acc is None else acc + gate * y
    o_ref[0] = acc.astype(BF16)


def _merge(ha, hb, hc, u, wa, wb, wc, bgate, layer):
    b_, n_, kd = ha.shape
    d = wa.shape[-1]
    tm = min(512, n_)
    tn = 1024
    nj = d // tn
    hspec = pl.BlockSpec((1, tm, kd), lambda b, i, j: (b, i, 0))
    gspec = lambda br: pl.BlockSpec((1, tm, tn), lambda b, i, j: (b, i, br * nj + j))
    wspec = pl.BlockSpec((None, kd, tn), lambda b, i, j: (layer, 0, j))
    return pl.pallas_call(
        _merge_kernel, grid=(b_, n_ // tm, nj),
        in_specs=[hspec, hspec, hspec, gspec(0), gspec(1), gspec(2), wspec, wspec, wspec,
                  pl.BlockSpec((3, 1, tn), lambda b, i, j: (0, 0, j))],
        out_specs=pl.BlockSpec((1, tm, tn), lambda b, i, j: (b, i, j)),
        out_shape=jax.ShapeDtypeStruct((b_, n_, d), BF16),
        compiler_params=_cparams(("arbitrary", "arbitrary", "arbitrary")),
        name="merge",
    )(ha, hb, hc, u, u, u, wa, wb, wc, bgate)


def _ffn_upact_kernel(x_ref, xp_ref, xn_ref, g_ref, sh_ref, sc_ref, wv_ref, wg_ref, cwv_ref, cwg_ref,
                      cbv_ref, cbg_ref, o_ref, h_ref, *, tm, rc):
    i = pl.program_id(1)
    last_i = pl.num_programs(1) - 1

    @pl.when(pl.program_id(2) == 0)
    def _():
        gain = g_ref[...]
        scale1 = 1.0 + sc_ref[0, 0]
        shift = sh_ref[0, 0]

        def normed(xs):
            ms = jnp.mean(xs * xs, axis=-1, keepdims=True)
            return xs * lax.rsqrt(ms + EPS) * gain * scale1 + shift

        h_ref[pl.ds(0, HALO), :] = jnp.where(i > 0, normed(xp_ref[0]), 0.0).astype(BF16)
        h_ref[pl.ds(HALO + tm, HALO), :] = jnp.where(i < last_i, normed(xn_ref[0]), 0.0).astype(BF16)

        def body(r, carry):
            xs = x_ref[0, pl.ds(pl.multiple_of(r * rc, rc), rc), :]
            h_ref[pl.ds(pl.multiple_of(HALO + r * rc, HALO), rc), :] = normed(xs).astype(BF16)
            return carry

        lax.fori_loop(0, tm // rc, body, 0)

    h = h_ref[...]

    rows = tm + 2 * HALO
    mid = slice(HALO, HALO + tm)

    def conv(w_ref, cw_ref, cb_ref):
        u = jnp.dot(h, w_ref[...], preferred_element_type=F32)
        prev = pltpu.roll(u, 1, 0)[mid]
        nxt = pltpu.roll(u, rows - 1, 0)[mid]
        return cw_ref[0:1, :] * prev + cw_ref[1:2, :] * u[mid] + cw_ref[2:3, :] * nxt + cb_ref[...]

    val = conv(wv_ref, cwv_ref, cbv_ref)
    gt = conv(wg_ref, cwg_ref, cbg_ref)
    o_ref[0] = (_silu(gt) * val).astype(BF16)


def _ffn_upact(x, wup, layer, conv_w, conv_b, gain, mod4, shi, sci, row0, rstr):
    b_, n_, d = x.shape
    dff = wup.shape[-1] // 2
    tm = min(512, n_)
    tf = _pick_tile(dff, 1408)
    nf = dff // tf
    nh = tm // HALO
    nblk = n_ // HALO
    kern = functools.partial(_ffn_upact_kernel, tm=tm, rc=min(128, tm))
    modspec = lambda which: pl.BlockSpec((1, 1, 1, d), lambda b, i, j: (row0 + b * rstr, which, 0, 0))
    wspec = lambda off: pl.BlockSpec((None, d, tf), lambda b, i, j: (layer, 0, off + j))
    cw = lambda off: pl.BlockSpec((3, tf), lambda b, i, j: (0, off + j))
    cb = lambda off: pl.BlockSpec((1, tf), lambda b, i, j: (0, off + j))
    return pl.pallas_call(
        kern, grid=(b_, n_ // tm, nf),
        in_specs=[pl.BlockSpec((1, tm, d), lambda b, i, j: (b, i, 0)),
                  pl.BlockSpec((1, HALO, d), lambda b, i, j: (b, jnp.maximum(i * nh - 1, 0), 0)),
                  pl.BlockSpec((1, HALO, d), lambda b, i, j: (b, jnp.minimum((i + 1) * nh, nblk - 1), 0)),
                  pl.BlockSpec((1, d), lambda b, i, j: (0, 0)),
                  modspec(shi), modspec(sci), wspec(0), wspec(nf), cw(0), cw(nf), cb(0), cb(nf)],
        out_specs=pl.BlockSpec((1, tm, tf), lambda b, i, j: (b, i, j)),
        out_shape=jax.ShapeDtypeStruct((b_, n_, dff), BF16),
        scratch_shapes=[pltpu.VMEM((tm + 2 * HALO, d), BF16)],
        compiler_params=_cparams(("arbitrary", "arbitrary", "arbitrary")),
        name="ffn_upact",
    )(x, x, x, gain, mod4, mod4, wup, wup, conv_w, conv_w, conv_b, conv_b)


def _rope_tables(n_tokens):
    rows = n_tokens // GRID_W
    row = jnp.broadcast_to(jnp.arange(rows, dtype=F32)[:, None], (rows, GRID_W)).reshape(-1)
    col = jnp.broadcast_to(jnp.arange(GRID_W, dtype=F32)[None, :], (rows, GRID_W)).reshape(-1)
    n_freq = DIFF_QK_DIM // 4
    inv = ROPE_BASE ** (-jnp.arange(n_freq, dtype=F32) / n_freq)
    ang = jnp.concatenate([row[:, None] * inv, col[:, None] * inv], axis=-1)
    cos, sin = jnp.cos(ang), jnp.sin(ang)
    cos_t = jnp.tile(cos, (1, LANES // cos.shape[1]))
    sin_t = jnp.tile(jnp.concatenate([-sin, sin], axis=-1), (1, LANES // (2 * sin.shape[1])))
    return cos_t, sin_t


def kernel(x, c, ctx, c_ctx, w_ada, b_ada, g_norm1, w_in, b_gate, conv_a_w, conv_a_b, ln_a_g, ln_a_b, w_a_out,
           qn_g, kn_g, lam_q1, lam_k1, lam_q2, lam_k2, subln_g, w_b_out, w_alpha2, b_alpha, gn_c_g, w_c_out,
           w_o, g_norm2, w_up, conv_f_w, conv_f_b, w_down):
    b_, n_lat, d = x.shape
    n_ctx = ctx.shape[1]
    depth = w_ada.shape[0]
    da = conv_a_w.shape[-1]
    qk_w = 2 * N_DIFF_HEADS * DIFF_QK_DIM
    db = w_b_out.shape[1]
    dc = w_c_out.shape[1]
    dk = w_alpha2.shape[-1]
    dff = w_down.shape[1]

    sizes = (2 * da, qk_w, qk_w, db, dk, dk, dc, dc, 2 * GATE_RANK, 3 * d)
    offs = np.concatenate([[0], np.cumsum(sizes)])
    o_a, o_qd, o_kd, o_vd, o_qg, o_kg, o_vg, o_r, o_lr, o_gate = (int(v) for v in offs[:-1])
    new_order = [(o_gate, 3 * d), (o_a, 2 * da), (o_qd, qk_w), (o_kd, qk_w), (o_vd, db), (o_vg, dc), (o_r, dc),
                 (o_qg, dk), (o_kg, dk), (o_lr, 2 * GATE_RANK)]
    starts = np.concatenate([[0], np.cumsum([s for _, s in new_order])])
    p_gate, p_a, p_qd, p_kd, p_vd, p_vg, p_r, p_qg, p_kg, p_lr = (int(v) for v in starts[:-1])
    tn_in = 768
    np_cols = -(-(p_lr + LANES) // tn_in) * tn_in
    assert p_a % (2 * da) == 0 and p_qd % qk_w == 0 and p_kd % qk_w == 0 and p_vd % LANES == 0
    assert p_vg % dc == 0 and p_r % dc == 0 and p_qg % dk == 0 and p_kg % dk == 0 and p_lr % LANES == 0

    def relayout_w_in(w):
        parts = [w[:, o:o + s] for o, s in new_order]
        parts.append(jnp.zeros((w.shape[0], np_cols - (p_lr + 2 * GATE_RANK)), w.dtype))
        return jnp.concatenate(parts, axis=1).astype(BF16)

    cos_l, sin_l = _rope_tables(n_lat)
    cos_c = jnp.ones((n_ctx, LANES), F32)
    sin_c = jnp.zeros((n_ctx, LANES), F32)
    grp = np.arange(LANES) // DIFF_QK_DIM
    gmat = jnp.asarray((grp[:, None] == grp[None, :]).astype(np.float32)).astype(BF16)

    cvecs = jnp.zeros((8, d), F32).at[:b_].set(c).at[b_].set(c_ctx)
    ctx_row = b_

    s_zero = jnp.zeros((b_, N_GLA_HEADS, dc // N_GLA_HEADS, dk // N_GLA_HEADS), F32)
    wa, wb, wc = w_a_out.astype(BF16), w_b_out.astype(BF16), w_c_out.astype(BF16)
    wo, wup, wdn = w_o.astype(BF16), w_up.astype(BF16), w_down.astype(BF16)

    for l in range(depth):
        last = l == depth - 1
        lam_init = 0.8 - 0.6 * math.exp(-0.3 * l)
        lam = (jnp.exp(jnp.sum(lam_q1[l] * lam_k1[l])) - jnp.exp(jnp.sum(lam_q2[l] * lam_k2[l])) + lam_init)
        lam_v = jnp.full((1, LANES), lam, F32)
        og = (subln_g[l] * (1.0 - lam_init)).reshape(1, LANES)
        gq = (jnp.tile(qn_g[l], LANES // DIFF_QK_DIM) * (DIFF_QK_DIM ** -0.5 * math.log2(math.e))).reshape(1, LANES)
        gk = jnp.tile(kn_g[l], LANES // DIFF_QK_DIM).reshape(1, LANES)

        mod = _adaln(cvecs, w_ada, b_ada, l)
        mod4 = mod.reshape(8, 6, 1, d)

        w_in_p = relayout_w_in(w_in[l])
        g1 =g_norm1[l].reshape(1, d)
        g2 = g_norm2[l].reshape(1, d)
        bg = b_gate[l].reshape(3, 1, d)
        w2 = w_alpha2[l]
        w2p = [jnp.zeros((LANES, dk), F32).at[z * GATE_RANK:(z + 1) * GATE_RANK].set(w2[z]) for z in range(2)]
        bal = [b_alpha[l, z].reshape(1, dk) for z in range(2)]
        gla_cbs = (p_qg // dk, p_kg // dk, p_vg // dc, p_lr // LANES, dk, dc)

        u_l = _mm(x, w_in_p, tn=2432, out_dtype=BF16, norm=(g1, mod4, 0, 1, 0, 1), name="in_proj")
        ctx_flat = ctx.reshape(1, b_ * n_ctx, d)
        u_c = _mm(ctx_flat, w_in_p, tn=2432, out_dtype=BF16, norm=(g1, mod4, 0, 1, ctx_row, 0),
                  name="in_proj_ctx").reshape(b_, n_ctx, np_cols)

        qp_l, kp_l = _qkprep(u_l, cos_l, sin_l, gq, gk, gmat, p_qd // qk_w, p_kd // qk_w)
        qp_c, kp_c = _qkprep(u_c, cos_c, sin_c, gq, gk, gmat, p_qd // qk_w, p_kd // qk_w)
        hb_l = _attn(qp_l, kp_l, u_l, p_vd // LANES, lam_v, og, ctx=(kp_c, u_c))

        of_c, sf = _gla_dir(u_c, gla_cbs, w2p[0], bal[0], s_zero, False)
        ob_c, sb = _gla_dir(u_c, gla_cbs, w2p[1], bal[1], s_zero, True)
        of_l, _ = _gla_dir(u_l, gla_cbs, w2p[0], bal[0], sf, False)
        ob_l, _ = _gla_dir(u_l, gla_cbs, w2p[1], bal[1], sb, True)
        gn = gn_c_g[l].reshape(1, -1)
        hc_l = _gla_out(of_l, ob_l, u_l, p_r // dc, gn)

        cb_a, lg_a, lb_a = conv_a_b[l].reshape(1, da), ln_a_g[l].reshape(1, da), ln_a_b[l].reshape(1, da)
        ha_l = _conformer(u_l, p_a // (2 * da), conv_a_w[l], cb_a, lg_a, lb_a)

        m_l = _merge(ha_l, hb_l, hc_l, u_l, wa, wb, wc, bg, l)
        x_mid = _mm(m_l, wo, tn=2048, out_dtype=F32, residual=(x, mod4, 2, 0, 1), layer=l, name="out_proj")

        cfb = conv_f_b[l].reshape(1, 2 * dff)
        act_l = _ffn_upact(x_mid, wup, l, conv_f_w[l], cfb, g2, mod4, 3, 4, 0, 1)
        x_new = _mm(act_l, wdn, tn=512, tm=1024, out_dtype=F32, residual=(x_mid, mod4, 5, 0, 1), layer=l,
                    name="ffn_down")

        if not last:
            hb_c = _attn(qp_c, kp_c, u_c, p_vd // LANES, lam_v, og)
            hc_c = _gla_out(of_c, ob_c, u_c, p_r // dc, gn)
            ha_c = _conformer(u_c, p_a // (2 * da), conv_a_w[l], cb_a, lg_a, lb_a)
            m_c = _merge(ha_c, hb_c, hc_c, u_c, wa, wb, wc, bg, l)
            ctx_mid = _mm(m_c.reshape(1, b_ * n_ctx, d), wo, tn=2048, out_dtype=F32,
                          residual=(ctx_flat, mod4, 2, ctx_row, 0), layer=l, name="out_proj_ctx")
            act_c = _ffn_upact(ctx_mid.reshape(b_, n_ctx, d), wup, l, conv_f_w[l], cfb, g2, mod4, 3, 4, ctx_row, 0)
            ctx = _mm(act_c.reshape(1, b_ * n_ctx, dff), wdn, tn=1024, out_dtype=F32,
                      residual=(ctx_mid, mod4, 5, ctx_row, 0), layer=l, name="ffn_down_ctx").reshape(b_, n_ctx, d)

        x = x_new

    return x
```

```python
import functools
import math

import numpy as np
import jax
import jax.numpy as jnp
from jax import lax
from jax.experimental import pallas as pl
from jax.experimental.pallas import tpu as pltpu

F32 = jnp.float32
BF16 = jnp.bfloat16

EPS = 1e-6
GRID_W = 64
ROPE_BASE = 10000.0
N_DIFF_HEADS = 8
DIFF_QK_DIM = 64
N_GLA_HEADS = 4
GATE_RANK = 16
GATE_TAU = 16.0
GLA_CHUNK = 64
CONV_A_W = 31

LANES = 128
VMEM_LIMIT_BYTES = 56 * 1024 * 1024

_NT = (((1,), (1,)), ((), ()))
_TN = (((0,), (0,)), ((), ()))


def _cparams(sem):
    return pltpu.CompilerParams(dimension_semantics=sem, vmem_limit_bytes=VMEM_LIMIT_BYTES)


def _pick_tile(total, target):
    best = LANES
    for t in range(LANES, min(total, target) + 1, LANES):
        if total % t == 0:
            best = t
    return best


def _sigmoid(x):
    return 1.0 / (1.0 + jnp.exp(-x))


def _silu(x):
    return x * _sigmoid(x)


def _adaln_kernel(c_ref, w_ref, b_ref, o_ref):
    a = _silu(c_ref[...])
    a_hi = a.astype(BF16)
    a_lo = (a - a_hi.astype(F32)).astype(BF16)
    w = w_ref[...]
    w_hi = w.astype(BF16)
    w_lo = (w - w_hi.astype(F32)).astype(BF16)
    rows = a.shape[0]
    r = jnp.dot(jnp.concatenate([a_hi, a_lo], axis=0), w_hi, preferred_element_type=F32)
    o_ref[...] = r[:rows] + r[rows:] + jnp.dot(a_hi, w_lo, preferred_element_type=F32) + b_ref[...]


def _adaln(cvecs, w_ada, b_ada, layer):
    rows, d = cvecs.shape
    n = w_ada.shape[-1]
    tn = 1024
    return pl.pallas_call(
        _adaln_kernel,
        grid=(n // tn,),
        in_specs=[pl.BlockSpec((rows, d), lambda j: (0, 0)),
                  pl.BlockSpec((None, d, tn), lambda j: (layer, 0, j)),
                  pl.BlockSpec((None, 1, tn), lambda j: (layer, 0, j))],
        out_specs=pl.BlockSpec((rows, tn), lambda j: (0, j)),
        out_shape=jax.ShapeDtypeStruct((rows, n), F32),
        compiler_params=_cparams(("arbitrary",)),
        name="adaln",
    )(cvecs, w_ada, b_ada.reshape(b_ada.shape[0], 1, n))


def _mm_kernel(*refs, norm, residual, tm, rc):
    it = iter(refs)
    x_ref = next(it)
    if norm:
        g_ref, sh_ref, sc_ref = next(it), next(it), next(it)
    w_ref = next(it)
    if residual:
        res_ref, gate_ref = next(it), next(it)
    o_ref = next(it)
    if norm:
        h_ref = next(it)

        @pl.when(pl.program_id(2) == 0)
        def _():
            gain = g_ref[...]
            scale1 = 1.0 + sc_ref[0, 0]
            shift = sh_ref[0, 0]

            def body(r, carry):
                xs = x_ref[0, pl.ds(pl.multiple_of(r * rc, rc), rc), :]
                ms = jnp.mean(xs * xs, axis=-1, keepdims=True)
                y = xs * lax.rsqrt(ms + EPS) * gain
                h_ref[pl.ds(pl.multiple_of(r * rc, rc), rc), :] = (y * scale1 + shift).astype(BF16)
                return carry

            lax.fori_loop(0, tm // rc, body, 0)

        lhs = h_ref[...]
    else:
        lhs = x_ref[0]
    acc = jnp.dot(lhs, w_ref[...], preferred_element_type=F32)
    if residual:
        o_ref[0] = res_ref[0] + gate_ref[0, 0] * acc
    else:
        o_ref[0] = acc.astype(o_ref.dtype)


def _mm(x, w, *, tn, out_dtype, norm=None, residual=None, layer=None, tm=512, name):
    g_, r_, k_ = x.shape
    n_ = w.shape[-1]
    tm = min(tm, r_)
    tn = _pick_tile(n_, tn)
    assert r_ % tm == 0
    grid = (g_, r_ // tm, n_ // tn)
    in_specs = [pl.BlockSpec((1, tm, k_), lambda g, i, j: (g, i, 0))]
    args = [x]
    scratch = []
    if norm is not None:
        gain, mod4, shi, sci, row0, rstr = norm
        in_specs += [pl.BlockSpec((1, k_), lambda g, i, j: (0, 0)),
                     pl.BlockSpec((1, 1, 1, k_), lambda g, i, j: (row0 + g * rstr, shi, 0, 0)),
                     pl.BlockSpec((1, 1, 1, k_), lambda g, i, j: (row0 + g * rstr, sci, 0, 0))]
        args += [gain, mod4, mod4]
        scratch = [pltpu.VMEM((tm, k_), BF16)]
    if layer is None:
        in_specs.append(pl.BlockSpec((k_, tn), lambda g, i, j: (0, j)))
    else:
        in_specs.append(pl.BlockSpec((None, k_, tn), lambda g, i, j: (layer, 0, j)))
    args.append(w)
    if residual is not None:
        res, mod4r, gti, rrow0, rrstr = residual
        in_specs += [pl.BlockSpec((1, tm, tn), lambda g, i, j: (g, i, j)),
                     pl.BlockSpec((1, 1, 1, tn), lambda g, i, j: (rrow0 + g * rrstr, gti, 0, j))]
        args += [res, mod4r]
    kern = functools.partial(_mm_kernel, norm=norm is not None, residual=residual is not None,
                             tm=tm, rc=min(128, tm))
    return pl.pallas_call(
        kern, grid=grid, in_specs=in_specs,
        out_specs=pl.BlockSpec((1, tm, tn), lambda g, i, j: (g, i, j)),
        out_shape=jax.ShapeDtypeStruct((g_, r_, n_), out_dtype),
        scratch_shapes=scratch,
        compiler_params=_cparams(("arbitrary", "arbitrary", "arbitrary")),
        name=name,
    )(*args)


def _qkprep_kernel(q_ref, k_ref, cos_ref, sin_ref, gq_ref, gk_ref, gm_ref, qo_ref, ko_ref, *, tm):
    cos = cos_ref[...]
    sin = sin_ref[...]
    gm = gm_ref[...]
    lane = lax.broadcasted_iota(jnp.int32, (tm, LANES), 1)
    first = (lane % DIFF_QK_DIM) < (DIFF_QK_DIM // 2)
    for src, g_ref, dst in ((q_ref, gq_ref, qo_ref), (k_ref, gk_ref, ko_ref)):
        gain = g_ref[...]
        for h in range(src.shape[-1] // LANES):
            sl = slice(h * LANES, (h + 1) * LANES)
            x = src[0, :, sl].astype(F32)
            sq = x * x
            hi = sq.astype(BF16)
            lo = (sq - hi.astype(F32)).astype(BF16)
            ss = (jnp.dot(hi, gm, preferred_element_type=F32) + jnp.dot(lo, gm, preferred_element_type=F32))
            y = x * lax.rsqrt(ss * (1.0 / DIFF_QK_DIM) + EPS) * gain
            partner = jnp.where(first, pltpu.roll(y, LANES - DIFF_QK_DIM // 2, 1),
                                pltpu.roll(y, DIFF_QK_DIM // 2, 1))
            dst[0, :, sl] = (y * cos + partner * sin).astype(BF16)


def _qkprep(u, cos_t, sin_t, gq, gk, gmat, qcb, kcb):
    b_, n_, _ = u.shape
    width = 2 * N_DIFF_HEADS * DIFF_QK_DIM
    tm = min(512, n_)
    kern = functools.partial(_qkprep_kernel, tm=tm)
    out = jax.ShapeDtypeStruct((b_, n_, width), BF16)
    return pl.pallas_call(
        kern, grid=(b_, n_ // tm),
        in_specs=[pl.BlockSpec((1, tm, width), lambda b, i: (b, i, qcb)),
                  pl.BlockSpec((1, tm, width), lambda b, i: (b, i, kcb)),
                  pl.BlockSpec((tm, LANES), lambda b, i: (i, 0)),
                  pl.BlockSpec((tm, LANES), lambda b, i: (i, 0)),
                  pl.BlockSpec((1, LANES), lambda b, i: (0, 0)),
                  pl.BlockSpec((1, LANES), lambda b, i: (0, 0)),
                  pl.BlockSpec((LANES, LANES), lambda b, i: (0, 0))],
        out_specs=[pl.BlockSpec((1, tm, width), lambda b, i: (b, i, 0)),
                   pl.BlockSpec((1, tm, width), lambda b, i: (b, i, 0))],
        out_shape=[out, out],
        compiler_params=_cparams(("arbitrary", "arbitrary")),
        name="qkprep",
    )(u, u, cos_t, sin_t, gq, gk, gmat)


def _attn_kernel(*refs, tq, kc, has_ctx):
    if has_ctx:
        q_ref, kl_ref, vl_ref, kc_ref, vc_ref, lam_ref, og_ref, o_ref, vxl_ref, vxc_ref = refs
    else:
        q_ref, kl_ref, vl_ref, lam_ref, og_ref, o_ref, vxl_ref = refs

    @pl.when(pl.program_id(2) == 0)
    def _():
        for v_ref, vx_ref in ((vl_ref, vxl_ref),) + (((vc_ref, vxc_ref),) if has_ctx else ()):
            vx_ref[:, :LANES] = v_ref[0]
            vx_ref[:, LANES:] = jnp.ones((vx_ref.shape[0], LANES), BF16)

    q = q_ref[0]
    lane = lax.broadcasted_iota(jnp.int32, (tq, LANES), 1)
    zero = jnp.zeros_like(q)
    qq = jnp.concatenate([jnp.where(lane < DIFF_QK_DIM, q, zero),
                          jnp.where(lane >= DIFF_QK_DIM, q, zero)], axis=0)
    chunks = [(kl_ref, vxl_ref, c * kc, min(kc, kl_ref.shape[1] - c * kc))
              for c in range(-(-kl_ref.shape[1] // kc))]
    if has_ctx:
        chunks += [(kc_ref, vxc_ref, c * kc, min(kc, kc_ref.shape[1] - c * kc))
                   for c in range(-(-kc_ref.shape[1] // kc))]
    m = None
    acc = None
    for k_ref, vx_ref, start, size in chunks:
        s = lax.dot_general(qq, k_ref[0, pl.ds(start, size), :], _NT, preferred_element_type=F32).astype(BF16)
        cmax = jnp.max(s, axis=-1, keepdims=True).astype(F32)
        m_new = cmax if m is None else jnp.maximum(m, cmax)
        p = jnp.exp2(s - m_new.astype(BF16))
        pv = jnp.dot(p, vx_ref[pl.ds(start, size), :], preferred_element_type=F32)
        acc = pv if acc is None else acc * jnp.exp2(m - m_new) + pv
        m = m_new
    lam = lam_ref[0:1, 0:1]
    o1 = acc[:tq, :LANES] / acc[:tq, LANES:LANES + 1]
    o2 = acc[tq:, :LANES] / acc[tq:, LANES:LANES + 1]
    o = o1 - lam * o2
    o = o * lax.rsqrt(jnp.mean(o * o, axis=-1, keepdims=True) + EPS) * og_ref[...]
    o_ref[0] = o.astype(BF16)


def _attn(qp, kp, u, vcb, lam_v, og, ctx=None):
    b_, n_, _ = qp.shape
    tq = min(512, n_)
    kc = 512
    has_ctx = ctx is not None
    in_specs = [pl.BlockSpec((1, tq, LANES), lambda b, h, i: (b, i, h)),
                pl.BlockSpec((1, n_, LANES), lambda b, h, i: (b, 0, h)),
                pl.BlockSpec((1, n_, LANES), lambda b, h, i: (b, 0, vcb + h))]
    args = [qp, kp, u]
    if has_ctx:
        kpc, uc = ctx
        nc = kpc.shape[1]
        in_specs += [pl.BlockSpec((1, nc, LANES), lambda b, h, i: (b, 0, h)),
                     pl.BlockSpec((1, nc, LANES), lambda b, h, i: (b, 0, vcb + h))]
        args += [kpc, uc]
    in_specs += [pl.BlockSpec((1, LANES), lambda b, h, i: (0, 0)),
                 pl.BlockSpec((1, LANES), lambda b, h, i: (0, 0))]
    args += [lam_v, og]
    scratch = [pltpu.VMEM((n_, 2 * LANES), BF16)]
    if has_ctx:
        scratch.append(pltpu.VMEM((ctx[0].shape[1], 2 * LANES), BF16))
    kern = functools.partial(_attn_kernel, tq=tq, kc=kc, has_ctx=has_ctx)
    return pl.pallas_call(
        kern, grid=(b_, N_DIFF_HEADS, n_ // tq), in_specs=in_specs,
        out_specs=pl.BlockSpec((1, tq, LANES), lambda b, h, i: (b, i, h)),
        out_shape=jax.ShapeDtypeStruct((b_, n_, N_DIFF_HEADS * LANES), BF16),
        scratch_shapes=scratch,
        compiler_params=_cparams(("arbitrary", "arbitrary", "arbitrary")),
        name="diff_attn_ctx" if has_ctx else "diff_attn",
    )(*args)


def _gla_constants(tt, reverse):
    c = GLA_CHUNK
    idx = np.arange(tt)
    same_chunk = (idx[:, None] // c) == (idx[None, :] // c)
    if reverse:
        cum = same_chunk & (idx[None, :] >= idx[:, None])
    else:
        cum = same_chunk & (idx[None, :] <= idx[:, None])
    tot = same_chunk
    sels, masks = [], []
    s = c // 2
    while s >= 1:
        blk = idx // (2 * s)
        right = (idx % (2 * s)) >= s
        same = blk[:, None] == blk[None, :]
        if reverse:
            ref = blk * 2 * s + s
            mask = same & (~right)[:, None] & right[None, :]
        else:
            ref = blk * 2 * s + s - 1
            mask = same & right[:, None] & (~right)[None, :]
        sels.append(idx[None, :] == ref[:, None])
        masks.append(mask)
        s //= 2
    masks.append(idx[:, None] == idx[None, :])
    cum_f = cum.astype(np.float32)
    dall = np.concatenate([cum_f, tot.astype(np.float32)]
                          + [cum_f - sel.astype(np.float32) @ cum_f for sel in sels], axis=0)
    return jnp.asarray(dall).astype(BF16), jnp.asarray(np.stack(masks).astype(np.float32))


def _gla_kernel(q_ref, k_ref, v_ref, lr_ref, w2h_ref, w2l_ref, b_ref, dall_ref, mask_ref, s0_ref,
                o_ref, sfin_ref, s_ref, *, tt, reverse, hk, hv):
    t = pl.program_id(1)

    @pl.when(t == 0)
    def _():
        s_ref[...] = s0_ref[0]

    n_lvl = mask_ref.shape[0] - 1
    lr = lr_ref[0]
    z = (jnp.dot(lr, w2h_ref[...], preferred_element_type=F32)
         + jnp.dot(lr, w2l_ref[...], preferred_element_type=F32) + b_ref[...])
    g = (jnp.minimum(z, 0.0) - jnp.log(1.0 + jnp.exp(-jnp.abs(z)))) * (1.0 / GATE_TAU)
    g_hi = g.astype(BF16)
    g_lo = (g - g_hi.astype(F32)).astype(BF16)
    sums = (jnp.dot(dall_ref[...], g_hi, preferred_element_type=F32)
            + jnp.dot(dall_ref[...], g_lo, preferred_element_type=F32))
    cum = sums[0:tt]
    tot = sums[tt:2 * tt]
    q = q_ref[0].astype(F32) * (hk ** -0.5)
    k = k_ref[0].astype(F32)
    eq = jnp.exp(cum)
    ek = jnp.exp(tot - cum)
    nch = tt // GLA_CHUNK
    order = range(nch - 1, -1, -1) if reverse else range(nch)
    for h in range(N_GLA_HEADS):
        sl = slice(h * hk, (h + 1) * hk)
        qh, kh = q[:, sl], k[:, sl]
        vh = v_ref[0, :, h * hv:(h + 1) * hv]
        a = mask_ref[n_lvl] * lax.dot_general(qh.astype(BF16), kh.astype(BF16), _NT, preferred_element_type=F32)
        for l in range(n_lvl):
            d = sums[(2 + l) * tt:(3 + l) * tt, sl]
            qt = (qh * jnp.exp(jnp.minimum(d, 0.0))).astype(BF16)
            kt = (kh * jnp.exp(jnp.minimum(-d, 0.0))).astype(BF16)
            a = a + mask_ref[l] * lax.dot_general(qt, kt, _NT, preferred_element_type=F32)
        o_h = jnp.dot(a.astype(BF16), vh, preferred_element_type=F32)
        qe = (qh * eq[:, sl]).astype(BF16)
        kd = (kh * ek[:, sl]).astype(BF16)
        outs = [None] * nch
        for c in order:
            rows = slice(c * GLA_CHUNK, (c + 1) * GLA_CHUNK)
            st = s_ref[h]
            outs[c] = o_h[rows] + lax.dot_general(qe[rows], st.astype(BF16), _NT, preferred_element_type=F32)
            dec = jnp.exp(tot[c * GLA_CHUNK:c * GLA_CHUNK + 1, sl])
            s_ref[h] = st * dec + lax.dot_general(vh[rows], kd[rows], _TN, preferred_element_type=F32)
        o_ref[0, :, h * hv:(h + 1) * hv] = jnp.concatenate(outs, axis=0) if nch > 1 else outs[0]

    @pl.when(t == pl.num_programs(1) - 1)
    def _():
        sfin_ref[0] = s_ref[...]


def _gla_dir(u, cbs, w2p, balpha, s0, reverse):
    b_, n_, _ = u.shape
    qcb, kcb, vcb, lrcb, dk, dv = cbs
    hk, hv = dk // N_GLA_HEADS, dv // N_GLA_HEADS
    tt = min(256, n_)
    nt = n_ // tt
    dall, mask = _gla_constants(tt, reverse)
    w2h = w2p.astype(BF16)
    w2l = (w2p - w2h.astype(F32)).astype(BF16)
    tmap = (lambda t: nt - 1 - t) if reverse else (lambda t: t)
    c2 = lambda b, t: (0, 0)
    c3 = lambda b, t: (0, 0, 0)
    kern = functools.partial(_gla_kernel, tt=tt, reverse=reverse, hk=hk, hv=hv)
    return pl.pallas_call(
        kern, grid=(b_, nt),
        in_specs=[pl.BlockSpec((1, tt, dk), lambda b, t: (b, tmap(t), qcb)),
                  pl.BlockSpec((1, tt, dk), lambda b, t: (b, tmap(t), kcb)),
                  pl.BlockSpec((1, tt, dv), lambda b, t: (b, tmap(t), vcb)),
                  pl.BlockSpec((1, tt, LANES), lambda b, t: (b, tmap(t), lrcb)),
                  pl.BlockSpec((LANES, dk), c2),
                  pl.BlockSpec((LANES, dk), c2),
                  pl.BlockSpec((1, dk), c2),
                  pl.BlockSpec(dall.shape, c2),
                  pl.BlockSpec(mask.shape, c3),
                  pl.BlockSpec((1, N_GLA_HEADS, hv, hk), lambda b, t: (b, 0, 0, 0))],
        out_specs=[pl.BlockSpec((1, tt, dv), lambda b, t: (b, tmap(t), 0)),
                   pl.BlockSpec((1, N_GLA_HEADS, hv, hk), lambda b, t: (b, 0, 0, 0))],
        out_shape=[jax.ShapeDtypeStruct((b_, n_, dv), F32),
                   jax.ShapeDtypeStruct((b_, N_GLA_HEADS, hv, hk), F32)],
        scratch_shapes=[pltpu.VMEM((N_GLA_HEADS, hv, hk), F32)],
        compiler_params=_cparams(("arbitrary", "arbitrary")),
        name="gla_bwd" if reverse else "gla_fwd",
    )(u, u, u, u, w2h, w2l, balpha, dall, mask, s0)


HALO = 16


def _conformer_kernel(a_ref, p_ref, n_ref, cw_ref, cb_ref, lg_ref, lb_ref, o_ref, h_ref, rot_ref, *, tm, da):
    i = pl.program_id(1)
    last = pl.num_programs(1) - 1

    def glu(a):
        a = a.astype(F32)
        return a[:, :da] * _sigmoid(a[:, da:])

    h_ref[pl.ds(0, HALO), :] = jnp.where(i > 0, glu(p_ref[0]), 0.0)
    h_ref[pl.ds(HALO, tm), :] = glu(a_ref[0])
    h_ref[pl.ds(HALO + tm, HALO), :] = jnp.where(i < last, glu(n_ref[0]), 0.0)
    pad = (CONV_A_W - 1) // 2
    acc = jnp.zeros((tm, da), F32) + cb_ref[...]
    sub = 8
    span = tm + 2 * HALO - sub
    for r in range(sub):
        if r > 0:
            rot_ref[...] = h_ref[pl.ds(r, span), :]
        src = rot_ref if r > 0 else h_ref
        for a in range(0, 2 * HALO, sub):
            k = a + r - (HALO - pad)
            if 0 <= k < CONV_A_W:
                acc = acc + cw_ref[k:k + 1, :] * src[pl.ds(a, tm), :]
    mu = jnp.mean(acc, axis=-1, keepdims=True)
    xc = acc - mu
    var = jnp.mean(xc * xc, axis=-1, keepdims=True)
    y = xc * lax.rsqrt(var + EPS) * lg_ref[...] + lb_ref[...]
    o_ref[0] = _silu(y).astype(BF16)


def _conformer(u, acb, conv_w, conv_b, ln_g, ln_b):
    b_, n_, _ = u.shape
    da = conv_w.shape[-1]
    tm = min(256, n_)
    nh = tm // HALO
    nblk = n_ // HALO
    kern = functools.partial(_conformer_kernel, tm=tm, da=da)
    row = lambda b, i: (0, 0)
    return pl.pallas_call(
        kern, grid=(b_, n_ // tm),
        in_specs=[pl.BlockSpec((1, tm, 2 * da), lambda b, i: (b, i, acb)),
                  pl.BlockSpec((1, HALO, 2 * da), lambda b, i: (b, jnp.maximum(i * nh - 1, 0), acb)),
                  pl.BlockSpec((1, HALO, 2 * da), lambda b, i: (b, jnp.minimum((i + 1) * nh, nblk - 1), acb)),
                  pl.BlockSpec((CONV_A_W, da), row),
                  pl.BlockSpec((1, da), row), pl.BlockSpec((1, da), row), pl.BlockSpec((1, da), row)],
        out_specs=pl.BlockSpec((1, tm, da), lambda b, i: (b, i, 0)),
        out_shape=jax.ShapeDtypeStruct((b_, n_, da), BF16),
        scratch_shapes=[pltpu.VMEM((tm + 2 * HALO, da), F32), pltpu.VMEM((tm + 2 * HALO - 8, da), F32)],
        compiler_params=_cparams(("arbitrary", "arbitrary")),
        name="conformer",
    )(u, u, u, conv_w, conv_b, ln_g, ln_b)


def _merge_kernel(ha_ref, hb_ref, of_ref, ob_ref, r_ref, gn_ref, ga_ref, gb_ref, gc_ref, wa_ref, wb_ref, wc_ref,
                  bg_ref, o_ref, hc_ref, *, hv):
    @pl.when(pl.program_id(2) == 0)
    def _():
        o = of_ref[0] + ob_ref[0]
        gate = _silu(r_ref[0].astype(F32))
        gn = gn_ref[...]
        for h in range(o.shape[-1] // hv):
            sl = slice(h * hv, (h + 1) * hv)
            oh = o[:, sl]
            y = oh * lax.rsqrt(jnp.mean(oh * oh, axis=-1, keepdims=True) + EPS) * gn
            hc_ref[:, sl] = (y * gate[:, sl]).astype(BF16)

    acc = None
    for idx, (lhs, g_ref, w_ref) in enumerate(((ha_ref[0], ga_ref, wa_ref), (hb_ref[0], gb_ref, wb_ref),
                                               (hc_ref[...], gc_ref, wc_ref))):
        y = jnp.dot(lhs, w_ref[...], preferred_element_type=F32)
        gate = _sigmoid(g_ref[0].astype(F32) + bg_ref[idx])
        acc = gate * y if acc is None else acc + gate * y
    o_ref[0] = acc.astype(BF16)


def _merge(ha, hb, o_f, o_b, u, rcb, gn, wa, wb, wc, bgate, layer):
    b_, n_, kd = ha.shape
    d = wa.shape[-1]
    tm = min(512, n_)
    tn = 1024
    nj = d // tn
    hv = kd // N_GLA_HEADS
    hspec = pl.BlockSpec((1, tm, kd), lambda b, i, j: (b, i, 0))
    gspec = lambda br: pl.BlockSpec((1, tm, tn), lambda b, i, j: (b, i, br * nj + j))
    wspec = pl.BlockSpec((None, kd, tn), lambda b, i, j: (layer, 0, j))
    return pl.pallas_call(
        functools.partial(_merge_kernel, hv=hv), grid=(b_, n_ // tm, nj),
        in_specs=[hspec, hspec, hspec, hspec,
                  pl.BlockSpec((1, tm, kd), lambda b, i, j: (b, i, rcb)),
                  pl.BlockSpec((1, hv), lambda b, i, j: (0, 0)),
                  gspec(0), gspec(1), gspec(2), wspec, wspec, wspec,
                  pl.BlockSpec((3, 1, tn), lambda b, i, j: (0, 0, j))],
        out_specs=pl.BlockSpec((1, tm, tn), lambda b, i, j: (b, i, j)),
        out_shape=jax.ShapeDtypeStruct((b_, n_, d), BF16),
        scratch_shapes=[pltpu.VMEM((tm, kd), BF16)],
        compiler_params=_cparams(("arbitrary", "arbitrary", "arbitrary")),
        name="merge",
    )(ha, hb, o_f, o_b, u, gn, u, u, u, wa, wb, wc, bgate)


def _ffn_upact_kernel(x_ref, xp_ref, xn_ref, g_ref, sh_ref, sc_ref, wv_ref, wg_ref, cwv_ref, cwg_ref,
                      cbv_ref, cbg_ref, o_ref, h_ref, *, tm, rc):
    i = pl.program_id(1)
    last_i = pl.num_programs(1) - 1

    @pl.when(pl.program_id(2) == 0)
    def _():
        gain = g_ref[...]
        scale1 = 1.0 + sc_ref[0, 0]
        shift = sh_ref[0, 0]

        def normed(xs):
            ms = jnp.mean(xs * xs, axis=-1, keepdims=True)
            return xs * lax.rsqrt(ms + EPS) * gain * scale1 + shift

        h_ref[pl.ds(0, HALO), :] = jnp.where(i > 0, normed(xp_ref[0]), 0.0).astype(BF16)
        h_ref[pl.ds(HALO + tm, HALO), :] = jnp.where(i < last_i, normed(xn_ref[0]), 0.0).astype(BF16)

        def body(r, carry):
            xs = x_ref[0, pl.ds(pl.multiple_of(r * rc, rc), rc), :]
            h_ref[pl.ds(pl.multiple_of(HALO + r * rc, HALO), rc), :] = normed(xs).astype(BF16)
            return carry

        lax.fori_loop(0, tm // rc, body, 0)

    h = h_ref[...]

    rows = tm + 2 * HALO
    mid = slice(HALO, HALO + tm)

    def conv(w_ref, cw_ref, cb_ref):
        u = jnp.dot(h, w_ref[...], preferred_element_type=F32)
        prev = pltpu.roll(u, 1, 0)[mid]
        nxt = pltpu.roll(u, rows - 1, 0)[mid]
        return cw_ref[0:1, :] * prev + cw_ref[1:2, :] * u[mid] + cw_ref[2:3, :] * nxt + cb_ref[...]

    val = conv(wv_ref, cwv_ref, cbv_ref)
    gt = conv(wg_ref, cwg_ref, cbg_ref)
    o_ref[0] = (_silu(gt) * val).astype(BF16)


def _ffn_upact(x, wup, layer, conv_w, conv_b, gain, mod4, shi, sci, row0, rstr):
    b_, n_, d = x.shape
    dff = wup.shape[-1] // 2
    tm = min(512, n_)
    tf = _pick_tile(dff, 1408)
    nf = dff // tf
    nh = tm // HALO
    nblk = n_ // HALO
    kern = functools.partial(_ffn_upact_kernel, tm=tm, rc=min(128, tm))
    modspec = lambda which: pl.BlockSpec((1, 1, 1, d), lambda b, i, j: (row0 + b * rstr, which, 0, 0))
    wspec = lambda off: pl.BlockSpec((None, d, tf), lambda b, i, j: (layer, 0, off + j))
    cw = lambda off: pl.BlockSpec((3, tf), lambda b, i, j: (0, off + j))
    cb = lambda off: pl.BlockSpec((1, tf), lambda b, i, j: (0, off + j))
    return pl.pallas_call(
        kern, grid=(b_, n_ // tm, nf),
        in_specs=[pl.BlockSpec((1, tm, d), lambda b, i, j: (b, i, 0)),
                  pl.BlockSpec((1, HALO, d), lambda b, i, j: (b, jnp.maximum(i * nh - 1, 0), 0)),
                  pl.BlockSpec((1, HALO, d), lambda b, i, j: (b, jnp.minimum((i + 1) * nh, nblk - 1), 0)),
                  pl.BlockSpec((1, d), lambda b, i, j: (0, 0)),
                  modspec(shi), modspec(sci), wspec(0), wspec(nf), cw(0), cw(nf), cb(0), cb(nf)],
        out_specs=pl.BlockSpec((1, tm, tf), lambda b, i, j: (b, i, j)),
        out_shape=jax.ShapeDtypeStruct((b_, n_, dff), BF16),
        scratch_shapes=[pltpu.VMEM((tm + 2 * HALO, d), BF16)],
        compiler_params=_cparams(("arbitrary", "arbitrary", "arbitrary")),
        name="ffn_upact",
    )(x, x, x, gain, mod4, mod4, wup, wup, conv_w, conv_w, conv_b, conv_b)


def _rope_tables(n_tokens):
    rows = n_tokens // GRID_W
    row = jnp.broadcast_to(jnp.arange(rows, dtype=F32)[:, None], (rows, GRID_W)).reshape(-1)
    col = jnp.broadcast_to(jnp.arange(GRID_W, dtype=F32)[None, :], (rows, GRID_W)).reshape(-1)
    n_freq = DIFF_QK_DIM // 4
    inv = ROPE_BASE ** (-jnp.arange(n_freq, dtype=F32) / n_freq)
    ang = jnp.concatenate([row[:, None] * inv, col[:, None] * inv], axis=-1)
    cos, sin = jnp.cos(ang), jnp.sin(ang)
    cos_t = jnp.tile(cos, (1, LANES // cos.shape[1]))
    sin_t = jnp.tile(jnp.concatenate([-sin, sin], axis=-1), (1, LANES // (2 * sin.shape[1])))
    return cos_t, sin_t


def kernel(x, c, ctx, c_ctx, w_ada, b_ada, g_norm1, w_in, b_gate, conv_a_w, conv_a_b, ln_a_g, ln_a_b, w_a_out,
           qn_g, kn_g, lam_q1, lam_k1, lam_q2, lam_k2, subln_g, w_b_out, w_alpha2, b_alpha, gn_c_g, w_c_out,
           w_o, g_norm2, w_up, conv_f_w, conv_f_b, w_down):
    b_, n_lat, d = x.shape
    n_ctx = ctx.shape[1]
    depth = w_ada.shape[0]
    da = conv_a_w.shape[-1]
    qk_w = 2 * N_DIFF_HEADS * DIFF_QK_DIM
    db = w_b_out.shape[1]
    dc = w_c_out.shape[1]
    dk = w_alpha2.shape[-1]
    dff = w_down.shape[1]

    sizes = (2 * da, qk_w, qk_w, db, dk, dk, dc, dc, 2 * GATE_RANK, 3 * d)
    offs = np.concatenate([[0], np.cumsum(sizes)])
    o_a, o_qd, o_kd, o_vd, o_qg, o_kg, o_vg, o_r, o_lr, o_gate = (int(v) for v in offs[:-1])
    new_order = [(o_gate, 3 * d), (o_a, 2 * da), (o_qd, qk_w), (o_kd, qk_w), (o_vd, db), (o_vg, dc), (o_r, dc),
                 (o_qg, dk), (o_kg, dk), (o_lr, 2 * GATE_RANK)]
    starts = np.concatenate([[0], np.cumsum([s for _, s in new_order])])
    p_gate, p_a, p_qd, p_kd, p_vd, p_vg, p_r, p_qg, p_kg, p_lr = (int(v) for v in starts[:-1])
    tn_in = 768
    np_cols = -(-(p_lr + LANES) // tn_in) * tn_in
    assert p_a % (2 * da) == 0 and p_qd % qk_w == 0 and p_kd % qk_w == 0 and p_vd % LANES == 0
    assert p_vg % dc == 0 and p_r % dc == 0 and p_qg % dk == 0 and p_kg % dk == 0 and p_lr % LANES == 0

    def relayout_w_in(w):
        parts = [w[:, o:o + s] for o, s in new_order]
        parts.append(jnp.zeros((w.shape[0], np_cols - (p_lr + 2 * GATE_RANK)), w.dtype))
        return jnp.concatenate(parts, axis=1).astype(BF16)

    cos_l, sin_l = _rope_tables(n_lat)
    cos_c = jnp.ones((n_ctx, LANES), F32)
    sin_c = jnp.zeros((n_ctx, LANES), F32)
    grp = np.arange(LANES) // DIFF_QK_DIM
    gmat = jnp.asarray((grp[:, None] == grp[None, :]).astype(np.float32)).astype(BF16)

    n_cond = 16
    cvecs = jnp.zeros((n_cond, d), F32).at[:b_].set(c).at[b_].set(c_ctx)
    ctx_row = b_

    s_zero = jnp.zeros((b_, N_GLA_HEADS, dc // N_GLA_HEADS, dk // N_GLA_HEADS), F32)
    wa, wb, wc = w_a_out.astype(BF16), w_b_out.astype(BF16), w_c_out.astype(BF16)
    wo, wup, wdn = w_o.astype(BF16), w_up.astype(BF16), w_down.astype(BF16)

    for l in range(depth):
        last = l == depth - 1
        lam_init = 0.8 - 0.6 * math.exp(-0.3 * l)
        lam = (jnp.exp(jnp.sum(lam_q1[l] * lam_k1[l])) - jnp.exp(jnp.sum(lam_q2[l] * lam_k2[l])) + lam_init)
        lam_v = jnp.full((1, LANES), lam, F32)
        og = (subln_g[l] * (1.0 - lam_init)).reshape(1, LANES)
        gq = (jnp.tile(qn_g[l], LANES // DIFF_QK_DIM) * (DIFF_QK_DIM ** -0.5 * math.log2(math.e))).reshape(1, LANES)
        gk = jnp.tile(kn_g[l], LANES // DIFF_QK_DIM).reshape(1, LANES)

        mod = _adaln(cvecs, w_ada, b_ada, l)
        mod4 = mod.reshape(n_cond, 6, 1, d)

        w_in_p = relayout_w_in(w_in[l])
        g1 =g_norm1[l].reshape(1, d)
        g2 = g_norm2[l].reshape(1, d)
        bg = b_gate[l].reshape(3, 1, d)
        w2 = w_alpha2[l]
        w2p = [jnp.zeros((LANES, dk), F32).at[z * GATE_RANK:(z + 1) * GATE_RANK].set(w2[z]) for z in range(2)]
        bal = [b_alpha[l, z].reshape(1, dk) for z in range(2)]
        gla_cbs = (p_qg // dk, p_kg // dk, p_vg // dc, p_lr // LANES, dk, dc)

        u_l = _mm(x, w_in_p, tn=2432, out_dtype=BF16, norm=(g1, mod4, 0, 1, 0, 1), name="in_proj")
        ctx_flat = ctx.reshape(1, b_ * n_ctx, d)
        u_c = _mm(ctx_flat, w_in_p, tn=2432, out_dtype=BF16, norm=(g1, mod4, 0, 1, ctx_row, 0),
                  name="in_proj_ctx").reshape(b_, n_ctx, np_cols)

        qp_l, kp_l = _qkprep(u_l, cos_l, sin_l, gq, gk, gmat, p_qd // qk_w, p_kd // qk_w)
        qp_c, kp_c = _qkprep(u_c, cos_c, sin_c, gq, gk, gmat, p_qd // qk_w, p_kd // qk_w)
        hb_l = _attn(qp_l, kp_l, u_l, p_vd // LANES, lam_v, og, ctx=(kp_c, u_c))

        of_c, sf = _gla_dir(u_c, gla_cbs, w2p[0], bal[0], s_zero, False)
        ob_c, sb = _gla_dir(u_c, gla_cbs, w2p[1], bal[1], s_zero, True)
        of_l, _ = _gla_dir(u_l, gla_cbs, w2p[0], bal[0], sf, False)
        ob_l, _ = _gla_dir(u_l, gla_cbs, w2p[1], bal[1], sb, True)
        gn = gn_c_g[l].reshape(1, -1)

        cb_a, lg_a, lb_a = conv_a_b[l].reshape(1, da), ln_a_g[l].reshape(1, da), ln_a_b[l].reshape(1, da)
        ha_l = _conformer(u_l, p_a // (2 * da), conv_a_w[l], cb_a, lg_a, lb_a)

        m_l = _merge(ha_l, hb_l, of_l, ob_l, u_l, p_r // dc, gn, wa, wb, wc, bg, l)
        x_mid = _mm(m_l, wo, tn=2048, out_dtype=F32, residual=(x, mod4, 2, 0, 1), layer=l, name="out_proj")

        cfb = conv_f_b[l].reshape(1, 2 * dff)
        act_l = _ffn_upact(x_mid, wup, l, conv_f_w[l], cfb, g2, mod4, 3, 4, 0, 1)
        x_new = _mm(act_l, wdn, tn=512, tm=1024, out_dtype=F32, residual=(x_mid, mod4, 5, 0, 1), layer=l,
                    name="ffn_down")

        if not last:
            hb_c = _attn(qp_c, kp_c, u_c, p_vd // LANES, lam_v, og)
            ha_c = _conformer(u_c, p_a // (2 * da), conv_a_w[l], cb_a, lg_a, lb_a)
            m_c = _merge(ha_c, hb_c, of_c, ob_c, u_c, p_r // dc, gn, wa, wb, wc, bg, l)
            ctx_mid = _mm(m_c.reshape(1, b_ * n_ctx, d), wo, tn=2048, out_dtype=F32,
                          residual=(ctx_flat, mod4, 2, ctx_row, 0), layer=l, name="out_proj_ctx")
            act_c = _ffn_upact(ctx_mid.reshape(b_, n_ctx, d), wup, l, conv_f_w[l], cfb, g2, mod4, 3, 4, ctx_row, 0)
            ctx = _mm(act_c.reshape(1, b_ * n_ctx, dff), wdn, tn=1024, out_dtype=F32,
                      residual=(ctx_mid, mod4, 5, ctx_row, 0), layer=l, name="ffn_down_ctx").reshape(b_, n_ctx, d)

        x = x_new

    return x
```

```python
import functools
import math

import numpy as np
import jax
import jax.numpy as jnp
from jax import lax
from jax.experimental import pallas as pl
from jax.experimental.pallas import tpu as pltpu

F32 = jnp.float32
BF16 = jnp.bfloat16

EPS = 1e-6
GRID_W = 64
ROPE_BASE = 10000.0
N_DIFF_HEADS = 8
DIFF_QK_DIM = 64
N_GLA_HEADS = 4
GATE_RANK = 16
GATE_TAU = 16.0
GLA_CHUNK = 64
CONV_A_W = 31

LANES = 128
VMEM_LIMIT_BYTES = 56 * 1024 * 1024

_NT = (((1,), (1,)), ((), ()))
_TN = (((0,), (0,)), ((), ()))


def _cparams(sem):
    return pltpu.CompilerParams(dimension_semantics=sem, vmem_limit_bytes=VMEM_LIMIT_BYTES)


def _pick_tile(total, target):
    best = LANES
    for t in range(LANES, min(total, target) + 1, LANES):
        if total % t == 0:
            best = t
    return best


def _sigmoid(x):
    return 1.0 / (1.0 + jnp.exp(-x))


def _silu(x):
    return x * _sigmoid(x)


def _adaln_kernel(c_ref, w_ref, b_ref, o_ref):
    a = _silu(c_ref[...])
    a_hi = a.astype(BF16)
    a_lo = (a - a_hi.astype(F32)).astype(BF16)
    w = w_ref[...]
    w_hi = w.astype(BF16)
    w_lo = (w - w_hi.astype(F32)).astype(BF16)
    rows = a.shape[0]
    r = jnp.dot(jnp.concatenate([a_hi, a_lo], axis=0), w_hi, preferred_element_type=F32)
    o_ref[...] = r[:rows] + r[rows:] + jnp.dot(a_hi, w_lo, preferred_element_type=F32) + b_ref[...]


def _adaln(cvecs, w_ada, b_ada, layer):
    rows, d = cvecs.shape
    n = w_ada.shape[-1]
    tn = 2048
    return pl.pallas_call(
        _adaln_kernel,
        grid=(n // tn,),
        in_specs=[pl.BlockSpec((rows, d), lambda j: (0, 0)),
                  pl.BlockSpec((None, d, tn), lambda j: (layer, 0, j)),
                  pl.BlockSpec((None, 1, tn), lambda j: (layer, 0, j))],
        out_specs=pl.BlockSpec((rows, tn), lambda j: (0, j)),
        out_shape=jax.ShapeDtypeStruct((rows, n), F32),
        compiler_params=_cparams(("arbitrary",)),
        name="adaln",
    )(cvecs, w_ada, b_ada.reshape(b_ada.shape[0], 1, n))


def _mm_kernel(*refs, norm, residual, tm, rc):
    it = iter(refs)
    x_ref = next(it)
    if norm:
        g_ref, sh_ref, sc_ref = next(it), next(it), next(it)
    w_ref = next(it)
    if residual:
        res_ref, gate_ref = next(it), next(it)
    o_ref = next(it)
    if norm:
        h_ref = next(it)

        @pl.when(pl.program_id(2) == 0)
        def _():
            gain = g_ref[...]
            scale1 = 1.0 + sc_ref[0, 0]
            shift = sh_ref[0, 0]

            def body(r, carry):
                xs = x_ref[0, pl.ds(pl.multiple_of(r * rc, rc), rc), :]
                ms = jnp.mean(xs * xs, axis=-1, keepdims=True)
                y = xs * lax.rsqrt(ms + EPS) * gain
                h_ref[pl.ds(pl.multiple_of(r * rc, rc), rc), :] = (y * scale1 + shift).astype(BF16)
                return carry

            lax.fori_loop(0, tm // rc, body, 0)

        lhs = h_ref[...]
    else:
        lhs = x_ref[0]
    acc = jnp.dot(lhs, w_ref[...], preferred_element_type=F32)
    if residual:
        o_ref[0] = res_ref[0] + gate_ref[0, 0] * acc
    else:
        o_ref[0] = acc.astype(o_ref.dtype)


def _mm(x, w, *, tn, out_dtype, norm=None, residual=None, layer=None, tm=512, name):
    g_, r_, k_ = x.shape
    n_ = w.shape[-1]
    tm = min(tm, r_)
    tn = _pick_tile(n_, tn)
    assert r_ % tm == 0
    grid = (g_, r_ // tm, n_ // tn)
    in_specs = [pl.BlockSpec((1, tm, k_), lambda g, i, j: (g, i, 0))]
    args = [x]
    scratch = []
    if norm is not None:
        gain, mod4, shi, sci, row0, rstr = norm
        in_specs += [pl.BlockSpec((1, k_), lambda g, i, j: (0, 0)),
                     pl.BlockSpec((1, 1, 1, k_), lambda g, i, j: (row0 + g * rstr, shi, 0, 0)),
                     pl.BlockSpec((1, 1, 1, k_), lambda g, i, j: (row0 + g * rstr, sci, 0, 0))]
        args += [gain, mod4, mod4]
        scratch = [pltpu.VMEM((tm, k_), BF16)]
    if layer is None:
        in_specs.append(pl.BlockSpec((k_, tn), lambda g, i, j: (0, j)))
    else:
        in_specs.append(pl.BlockSpec((None, k_, tn), lambda g, i, j: (layer, 0, j)))
    args.append(w)
    if residual is not None:
        res, mod4r, gti, rrow0, rrstr = residual
        in_specs += [pl.BlockSpec((1, tm, tn), lambda g, i, j: (g, i, j)),
                     pl.BlockSpec((1, 1, 1, tn), lambda g, i, j: (rrow0 + g * rrstr, gti, 0, j))]
        args += [res, mod4r]
    kern = functools.partial(_mm_kernel, norm=norm is not None, residual=residual is not None,
                             tm=tm, rc=min(128, tm))
    return pl.pallas_call(
        kern, grid=grid, in_specs=in_specs,
        out_specs=pl.BlockSpec((1, tm, tn), lambda g, i, j: (g, i, j)),
        out_shape=jax.ShapeDtypeStruct((g_, r_, n_), out_dtype),
        scratch_shapes=scratch,
        compiler_params=_cparams(("arbitrary", "arbitrary", "arbitrary")),
        name=name,
    )(*args)


def _qkprep_kernel(q_ref, k_ref, cos_ref, sin_ref, gq_ref, gk_ref, gm_ref, qo_ref, ko_ref, *, tm):
    cos = cos_ref[...]
    sin = sin_ref[...]
    gm = gm_ref[...]
    lane = lax.broadcasted_iota(jnp.int32, (tm, LANES), 1)
    first = (lane % DIFF_QK_DIM) < (DIFF_QK_DIM // 2)
    for src, g_ref, dst in ((q_ref, gq_ref, qo_ref), (k_ref, gk_ref, ko_ref)):
        gain = g_ref[...]
        for h in range(src.shape[-1] // LANES):
            sl = slice(h * LANES, (h + 1) * LANES)
            x = src[0, :, sl].astype(F32)
            sq = x * x
            hi = sq.astype(BF16)
            lo = (sq - hi.astype(F32)).astype(BF16)
            ss = (jnp.dot(hi, gm, preferred_element_type=F32) + jnp.dot(lo, gm, preferred_element_type=F32))
            y = x * lax.rsqrt(ss * (1.0 / DIFF_QK_DIM) + EPS) * gain
            partner = jnp.where(first, pltpu.roll(y, LANES - DIFF_QK_DIM // 2, 1),
                                pltpu.roll(y, DIFF_QK_DIM // 2, 1))
            dst[0, :, sl] = (y * cos + partner * sin).astype(BF16)


def _qkprep(u, cos_t, sin_t, gq, gk, gmat, qcb, kcb):
    b_, n_, _ = u.shape
    width = 2 * N_DIFF_HEADS * DIFF_QK_DIM
    tm = min(512, n_)
    kern = functools.partial(_qkprep_kernel, tm=tm)
    out = jax.ShapeDtypeStruct((b_, n_, width), BF16)
    return pl.pallas_call(
        kern, grid=(b_, n_ // tm),
        in_specs=[pl.BlockSpec((1, tm, width), lambda b, i: (b, i, qcb)),
                  pl.BlockSpec((1, tm, width), lambda b, i: (b, i, kcb)),
                  pl.BlockSpec((tm, LANES), lambda b, i: (i, 0)),
                  pl.BlockSpec((tm, LANES), lambda b, i: (i, 0)),
                  pl.BlockSpec((1, LANES), lambda b, i: (0, 0)),
                  pl.BlockSpec((1, LANES), lambda b, i: (0, 0)),
                  pl.BlockSpec((LANES, LANES), lambda b, i: (0, 0))],
        out_specs=[pl.BlockSpec((1, tm, width), lambda b, i: (b, i, 0)),
                   pl.BlockSpec((1, tm, width), lambda b, i: (b, i, 0))],
        out_shape=[out, out],
        compiler_params=_cparams(("arbitrary", "arbitrary")),
        name="qkprep",
    )(u, u, cos_t, sin_t, gq, gk, gmat)


def _attn_kernel(*refs, tq, kc, has_ctx):
    if has_ctx:
        q_ref, kl_ref, vl_ref, kc_ref, vc_ref, lam_ref, og_ref, o_ref, vxl_ref, vxc_ref = refs
    else:
        q_ref, kl_ref, vl_ref, lam_ref, og_ref, o_ref, vxl_ref = refs

    @pl.when(pl.program_id(2) == 0)
    def _():
        for v_ref, vx_ref in ((vl_ref, vxl_ref),) + (((vc_ref, vxc_ref),) if has_ctx else ()):
            vx_ref[:, :LANES] = v_ref[0]
            vx_ref[:, LANES:] = jnp.ones((vx_ref.shape[0], LANES), BF16)

    q = q_ref[0]
    lane = lax.broadcasted_iota(jnp.int32, (tq, LANES), 1)
    zero = jnp.zeros_like(q)
    qq = jnp.concatenate([jnp.where(lane < DIFF_QK_DIM, q, zero),
                          jnp.where(lane >= DIFF_QK_DIM, q, zero)], axis=0)
    chunks = [(kl_ref, vxl_ref, c * kc, min(kc, kl_ref.shape[1] - c * kc))
              for c in range(-(-kl_ref.shape[1] // kc))]
    if has_ctx:
        chunks += [(kc_ref, vxc_ref, c * kc, min(kc, kc_ref.shape[1] - c * kc))
                   for c in range(-(-kc_ref.shape[1] // kc))]
    m = None
    acc = None
    for k_ref, vx_ref, start, size in chunks:
        s = lax.dot_general(qq, k_ref[0, pl.ds(start, size), :], _NT, preferred_element_type=F32).astype(BF16)
        cmax = jnp.max(s, axis=-1, keepdims=True).astype(F32)
        m_new = cmax if m is None else jnp.maximum(m, cmax)
        p = jnp.exp2(s - m_new.astype(BF16))
        pv = jnp.dot(p, vx_ref[pl.ds(start, size), :], preferred_element_type=F32)
        acc = pv if acc is None else acc * jnp.exp2(m - m_new) + pv
        m = m_new
    lam = lam_ref[0:1, 0:1]
    o1 = acc[:tq, :LANES] / acc[:tq, LANES:LANES + 1]
    o2 = acc[tq:, :LANES] / acc[tq:, LANES:LANES + 1]
    o = o1 - lam * o2
    o = o * lax.rsqrt(jnp.mean(o * o, axis=-1, keepdims=True) + EPS) * og_ref[...]
    o_ref[0] = o.astype(BF16)


def _attn(qp, kp, u, vcb, lam_v, og, ctx=None):
    b_, n_, _ = qp.shape
    tq = min(512, n_)
    kc = 512
    has_ctx = ctx is not None
    in_specs = [pl.BlockSpec((1, tq, LANES), lambda b, h, i: (b, i, h)),
                pl.BlockSpec((1, n_, LANES), lambda b, h, i: (b, 0, h)),
                pl.BlockSpec((1, n_, LANES), lambda b, h, i: (b, 0, vcb + h))]
    args = [qp, kp, u]
    if has_ctx:
        kpc, uc = ctx
        nc = kpc.shape[1]
        in_specs += [pl.BlockSpec((1, nc, LANES), lambda b, h, i: (b, 0, h)),
                     pl.BlockSpec((1, nc, LANES), lambda b, h, i: (b, 0, vcb + h))]
        args += [kpc, uc]
    in_specs += [pl.BlockSpec((1, LANES), lambda b, h, i: (0, 0)),
                 pl.BlockSpec((1, LANES), lambda b, h, i: (0, 0))]
    args += [lam_v, og]
    scratch = [pltpu.VMEM((n_, 2 * LANES), BF16)]
    if has_ctx:
        scratch.append(pltpu.VMEM((ctx[0].shape[1], 2 * LANES), BF16))
    kern = functools.partial(_attn_kernel, tq=tq, kc=kc, has_ctx=has_ctx)
    return pl.pallas_call(
        kern, grid=(b_, N_DIFF_HEADS, n_ // tq), in_specs=in_specs,
        out_specs=pl.BlockSpec((1, tq, LANES), lambda b, h, i: (b, i, h)),
        out_shape=jax.ShapeDtypeStruct((b_, n_, N_DIFF_HEADS * LANES), BF16),
        scratch_shapes=scratch,
        compiler_params=_cparams(("arbitrary", "arbitrary", "arbitrary")),
        name="diff_attn_ctx" if has_ctx else "diff_attn",
    )(*args)


def _gla_constants(tt, reverse):
    c = GLA_CHUNK
    idx = np.arange(tt)
    same_chunk = (idx[:, None] // c) == (idx[None, :] // c)
    if reverse:
        cum = same_chunk & (idx[None, :] >= idx[:, None])
    else:
        cum = same_chunk & (idx[None, :] <= idx[:, None])
    tot = same_chunk
    sels, masks = [], []
    s = c // 2
    while s >= 1:
        blk = idx // (2 * s)
        right = (idx % (2 * s)) >= s
        same = blk[:, None] == blk[None, :]
        if reverse:
            ref = blk * 2 * s + s
            mask = same & (~right)[:, None] & right[None, :]
        else:
            ref = blk * 2 * s + s - 1
            mask = same & right[:, None] & (~right)[None, :]
        sels.append(idx[None, :] == ref[:, None])
        masks.append(mask)
        s //= 2
    masks.append(idx[:, None] == idx[None, :])
    cum_f = cum.astype(np.float32)
    dall = np.concatenate([cum_f, tot.astype(np.float32)]
                          + [cum_f - sel.astype(np.float32) @ cum_f for sel in sels], axis=0)
    return jnp.asarray(dall).astype(BF16), jnp.asarray(np.stack(masks).astype(np.float32))


def _gla_kernel(q_ref, k_ref, v_ref, lr_ref, w2h_ref, w2l_ref, b_ref, dall_ref, mask_ref, s0_ref,
                o_ref, sfin_ref, s_ref, *, tt, reverse, hk, hv):
    t = pl.program_id(1)

    @pl.when(t == 0)
    def _():
        s_ref[...] = s0_ref[0]

    n_lvl = mask_ref.shape[0] - 1
    lr = lr_ref[0]
    z = (jnp.dot(lr, w2h_ref[...], preferred_element_type=F32)
         + jnp.dot(lr, w2l_ref[...], preferred_element_type=F32) + b_ref[...])
    g = (jnp.minimum(z, 0.0) - jnp.log(1.0 + jnp.exp(-jnp.abs(z)))) * (1.0 / GATE_TAU)
    g_hi = g.astype(BF16)
    g_lo = (g - g_hi.astype(F32)).astype(BF16)
    sums = (jnp.dot(dall_ref[...], g_hi, preferred_element_type=F32)
            + jnp.dot(dall_ref[...], g_lo, preferred_element_type=F32))
    cum = sums[0:tt]
    tot = sums[tt:2 * tt]
    q = q_ref[0].astype(F32) * (hk ** -0.5)
    k = k_ref[0].astype(F32)
    eq = jnp.exp(cum)
    ek = jnp.exp(tot - cum)
    nch = tt // GLA_CHUNK
    order = range(nch - 1, -1, -1) if reverse else range(nch)
    for h in range(N_GLA_HEADS):
        sl = slice(h * hk, (h + 1) * hk)
        qh, kh = q[:, sl], k[:, sl]
        vh = v_ref[0, :, h * hv:(h + 1) * hv]
        a = mask_ref[n_lvl] * lax.dot_general(qh.astype(BF16), kh.astype(BF16), _NT, preferred_element_type=F32)
        for l in range(n_lvl):
            d = sums[(2 + l) * tt:(3 + l) * tt, sl]
            qt = (qh * jnp.exp(jnp.minimum(d, 0.0))).astype(BF16)
            kt = (kh * jnp.exp(jnp.minimum(-d, 0.0))).astype(BF16)
            a = a + mask_ref[l] * lax.dot_general(qt, kt, _NT, preferred_element_type=F32)
        o_h = jnp.dot(a.astype(BF16), vh, preferred_element_type=F32)
        qe = (qh * eq[:, sl]).astype(BF16)
        kd = (kh * ek[:, sl]).astype(BF16)
        outs = [None] * nch
        for c in order:
            rows = slice(c * GLA_CHUNK, (c + 1) * GLA_CHUNK)
            st = s_ref[h]
            outs[c] = o_h[rows] + lax.dot_general(qe[rows], st.astype(BF16), _NT, preferred_element_type=F32)
            dec = jnp.exp(tot[c * GLA_CHUNK:c * GLA_CHUNK + 1, sl])
            s_ref[h] = st * dec + lax.dot_general(vh[rows], kd[rows], _TN, preferred_element_type=F32)
        o_ref[0, :, h * hv:(h + 1) * hv] = jnp.concatenate(outs, axis=0) if nch > 1 else outs[0]

    @pl.when(t == pl.num_programs(1) - 1)
    def _():
        sfin_ref[0] = s_ref[...]


def _gla_dir(u, cbs, w2p, balpha, s0, reverse):
    b_, n_, _ = u.shape
    qcb, kcb, vcb, lrcb, dk, dv = cbs
    hk, hv = dk // N_GLA_HEADS, dv // N_GLA_HEADS
    tt = min(256, n_)
    nt = n_ // tt
    dall, mask = _gla_constants(tt, reverse)
    w2h = w2p.astype(BF16)
    w2l = (w2p - w2h.astype(F32)).astype(BF16)
    tmap = (lambda t: nt - 1 - t) if reverse else (lambda t: t)
    c2 = lambda b, t: (0, 0)
    c3 = lambda b, t: (0, 0, 0)
    kern = functools.partial(_gla_kernel, tt=tt, reverse=reverse, hk=hk, hv=hv)
    return pl.pallas_call(
        kern, grid=(b_, nt),
        in_specs=[pl.BlockSpec((1, tt, dk), lambda b, t: (b, tmap(t), qcb)),
                  pl.BlockSpec((1, tt, dk), lambda b, t: (b, tmap(t), kcb)),
                  pl.BlockSpec((1, tt, dv), lambda b, t: (b, tmap(t), vcb)),
                  pl.BlockSpec((1, tt, LANES), lambda b, t: (b, tmap(t), lrcb)),
                  pl.BlockSpec((LANES, dk), c2),
                  pl.BlockSpec((LANES, dk), c2),
                  pl.BlockSpec((1, dk), c2),
                  pl.BlockSpec(dall.shape, c2),
                  pl.BlockSpec(mask.shape, c3),
                  pl.BlockSpec((1, N_GLA_HEADS, hv, hk), lambda b, t: (b, 0, 0, 0))],
        out_specs=[pl.BlockSpec((1, tt, dv), lambda b, t: (b, tmap(t), 0)),
                   pl.BlockSpec((1, N_GLA_HEADS, hv, hk), lambda b, t: (b, 0, 0, 0))],
        out_shape=[jax.ShapeDtypeStruct((b_, n_, dv), F32),
                   jax.ShapeDtypeStruct((b_, N_GLA_HEADS, hv, hk), F32)],
        scratch_shapes=[pltpu.VMEM((N_GLA_HEADS, hv, hk), F32)],
        compiler_params=_cparams(("arbitrary", "arbitrary")),
        name="gla_bwd" if reverse else "gla_fwd",
    )(u, u, u, u, w2h, w2l, balpha, dall, mask, s0)


HALO = 16


def _conformer_kernel(a_ref, p_ref, n_ref, cw_ref, cb_ref, lg_ref, lb_ref, o_ref, h_ref, rot_ref, *, tm, da):
    i = pl.program_id(1)
    last = pl.num_programs(1) - 1

    def glu(a):
        a = a.astype(F32)
        return a[:, :da] * _sigmoid(a[:, da:])

    h_ref[pl.ds(0, HALO), :] = jnp.where(i > 0, glu(p_ref[0]), 0.0)
    h_ref[pl.ds(HALO, tm), :] = glu(a_ref[0])
    h_ref[pl.ds(HALO + tm, HALO), :] = jnp.where(i < last, glu(n_ref[0]), 0.0)
    pad = (CONV_A_W - 1) // 2
    acc = jnp.zeros((tm, da), F32) + cb_ref[...]
    sub = 8
    span = tm + 2 * HALO - sub
    for r in range(sub):
        if r > 0:
            rot_ref[...] = h_ref[pl.ds(r, span), :]
        src = rot_ref if r > 0 else h_ref
        for a in range(0, 2 * HALO, sub):
            k = a + r - (HALO - pad)
            if 0 <= k < CONV_A_W:
                acc = acc + cw_ref[k:k + 1, :] * src[pl.ds(a, tm), :]
    mu = jnp.mean(acc, axis=-1, keepdims=True)
    xc = acc - mu
    var = jnp.mean(xc * xc, axis=-1, keepdims=True)
    y = xc * lax.rsqrt(var + EPS) * lg_ref[...] + lb_ref[...]
    o_ref[0] = _silu(y).astype(BF16)


def _conformer(u, acb, conv_w, conv_b, ln_g, ln_b):
    b_, n_, _ = u.shape
    da = conv_w.shape[-1]
    tm = min(256, n_)
    nh = tm // HALO
    nblk = n_ // HALO
    kern = functools.partial(_conformer_kernel, tm=tm, da=da)
    row = lambda b, i: (0, 0)
    return pl.pallas_call(
        kern, grid=(b_, n_ // tm),
        in_specs=[pl.BlockSpec((1, tm, 2 * da), lambda b, i: (b, i, acb)),
                  pl.BlockSpec((1, HALO, 2 * da), lambda b, i: (b, jnp.maximum(i * nh - 1, 0), acb)),
                  pl.BlockSpec((1, HALO, 2 * da), lambda b, i: (b, jnp.minimum((i + 1) * nh, nblk - 1), acb)),
                  pl.BlockSpec((CONV_A_W, da), row),
                  pl.BlockSpec((1, da), row), pl.BlockSpec((1, da), row), pl.BlockSpec((1, da), row)],
        out_specs=pl.BlockSpec((1, tm, da), lambda b, i: (b, i, 0)),
        out_shape=jax.ShapeDtypeStruct((b_, n_, da), BF16),
        scratch_shapes=[pltpu.VMEM((tm + 2 * HALO, da), F32), pltpu.VMEM((tm + 2 * HALO - 8, da), F32)],
        compiler_params=_cparams(("arbitrary", "arbitrary")),
        name="conformer",
    )(u, u, u, conv_w, conv_b, ln_g, ln_b)


def _merge_kernel(ha_ref, hb_ref, of_ref, ob_ref, r_ref, gn_ref, ga_ref, gb_ref, gc_ref, wa_ref, wb_ref, wc_ref,
                  bg_ref, o_ref, hc_ref, *, hv):
    @pl.when(pl.program_id(2) == 0)
    def _():
        o = of_ref[0] + ob_ref[0]
        gate = _silu(r_ref[0].astype(F32))
        gn = gn_ref[...]
        for h in range(o.shape[-1] // hv):
            sl = slice(h * hv, (h + 1) * hv)
            oh = o[:, sl]
            y = oh * lax.rsqrt(jnp.mean(oh * oh, axis=-1, keepdims=True) + EPS) * gn
            hc_ref[:, sl] = (y * gate[:, sl]).astype(BF16)

    acc = None
    for idx, (lhs, g_ref, w_ref) in enumerate(((ha_ref[0], ga_ref, wa_ref), (hb_ref[0], gb_ref, wb_ref),
                                               (hc_ref[...], gc_ref, wc_ref))):
        y = jnp.dot(lhs, w_ref[...], preferred_element_type=F32)
        gate = _sigmoid(g_ref[0].astype(F32) + bg_ref[idx])
        acc = gate * y if acc is None else acc + gate * y
    o_ref[0] = acc.astype(BF16)


def _merge(ha, hb, o_f, o_b, u, rcb, gn, wa, wb, wc, bgate, layer):
    b_, n_, kd = ha.shape
    d = wa.shape[-1]
    tm = min(512, n_)
    tn = 1024
    nj = d // tn
    hv = kd // N_GLA_HEADS
    hspec = pl.BlockSpec((1, tm, kd), lambda b, i, j: (b, i, 0))
    gspec = lambda br: pl.BlockSpec((1, tm, tn), lambda b, i, j: (b, i, br * nj + j))
    wspec = pl.BlockSpec((None, kd, tn), lambda b, i, j: (layer, 0, j))
    return pl.pallas_call(
        functools.partial(_merge_kernel, hv=hv), grid=(b_, n_ // tm, nj),
        in_specs=[hspec, hspec, hspec, hspec,
                  pl.BlockSpec((1, tm, kd), lambda b, i, j: (b, i, rcb)),
                  pl.BlockSpec((1, hv), lambda b, i, j: (0, 0)),
                  gspec(0), gspec(1), gspec(2), wspec, wspec, wspec,
                  pl.BlockSpec((3, 1, tn), lambda b, i, j: (0, 0, j))],
        out_specs=pl.BlockSpec((1, tm, tn), lambda b, i, j: (b, i, j)),
        out_shape=jax.ShapeDtypeStruct((b_, n_, d), BF16),
        scratch_shapes=[pltpu.VMEM((tm, kd), BF16)],
        compiler_params=_cparams(("arbitrary", "arbitrary", "arbitrary")),
        name="merge",
    )(ha, hb, o_f, o_b, u, gn, u, u, u, wa, wb, wc, bgate)


def _ffn_upact_kernel(x_ref, xp_ref, xn_ref, g_ref, sh_ref, sc_ref, wv_ref, wg_ref, cwv_ref, cwg_ref,
                      cbv_ref, cbg_ref, o_ref, h_ref, *, tm, rc):
    i = pl.program_id(1)
    last_i = pl.num_programs(1) - 1

    @pl.when(pl.program_id(2) == 0)
    def _():
        gain = g_ref[...]
        scale1 = 1.0 + sc_ref[0, 0]
        shift = sh_ref[0, 0]

        def normed(xs):
            ms = jnp.mean(xs * xs, axis=-1, keepdims=True)
            return xs * lax.rsqrt(ms + EPS) * gain * scale1 + shift

        h_ref[pl.ds(0, HALO), :] = jnp.where(i > 0, normed(xp_ref[0]), 0.0).astype(BF16)
        h_ref[pl.ds(HALO + tm, HALO), :] = jnp.where(i < last_i, normed(xn_ref[0]), 0.0).astype(BF16)

        def body(r, carry):
            xs = x_ref[0, pl.ds(pl.multiple_of(r * rc, rc), rc), :]
            h_ref[pl.ds(pl.multiple_of(HALO + r * rc, HALO), rc), :] = normed(xs).astype(BF16)
            return carry

        lax.fori_loop(0, tm // rc, body, 0)

    h = h_ref[...]

    rows = tm + 2 * HALO
    mid = slice(HALO, HALO + tm)

    def conv(w_ref, cw_ref, cb_ref):
        u = jnp.dot(h, w_ref[...], preferred_element_type=F32)
        prev = pltpu.roll(u, 1, 0)[mid]
        nxt = pltpu.roll(u, rows - 1, 0)[mid]
        return cw_ref[0:1, :] * prev + cw_ref[1:2, :] * u[mid] + cw_ref[2:3, :] * nxt + cb_ref[...]

    val = conv(wv_ref, cwv_ref, cbv_ref)
    gt = conv(wg_ref, cwg_ref, cbg_ref)
    o_ref[0] = (_silu(gt) * val).astype(BF16)


def _ffn_upact(x, wup, layer, conv_w, conv_b, gain, mod4, shi, sci, row0, rstr):
    b_, n_, d = x.shape
    dff = wup.shape[-1] // 2
    tm = min(512, n_)
    tf = _pick_tile(dff, 1408)
    nf = dff // tf
    nh = tm // HALO
    nblk = n_ // HALO
    kern = functools.partial(_ffn_upact_kernel, tm=tm, rc=min(128, tm))
    modspec = lambda which: pl.BlockSpec((1, 1, 1, d), lambda b, i, j: (row0 + b * rstr, which, 0, 0))
    wspec = lambda off: pl.BlockSpec((None, d, tf), lambda b, i, j: (layer, 0, off + j))
    cw = lambda off: pl.BlockSpec((3, tf), lambda b, i, j: (0, off + j))
    cb = lambda off: pl.BlockSpec((1, tf), lambda b, i, j: (0, off + j))
    return pl.pallas_call(
        kern, grid=(b_, n_ // tm, nf),
        in_specs=[pl.BlockSpec((1, tm, d), lambda b, i, j: (b, i, 0)),
                  pl.BlockSpec((1, HALO, d), lambda b, i, j: (b, jnp.maximum(i * nh - 1, 0), 0)),
                  pl.BlockSpec((1, HALO, d), lambda b, i, j: (b, jnp.minimum((i + 1) * nh, nblk - 1), 0)),
                  pl.BlockSpec((1, d), lambda b, i, j: (0, 0)),
                  modspec(shi), modspec(sci), wspec(0), wspec(nf), cw(0), cw(nf), cb(0), cb(nf)],
        out_specs=pl.BlockSpec((1, tm, tf), lambda b, i, j: (b, i, j)),
        out_shape=jax.ShapeDtypeStruct((b_, n_, dff), BF16),
        scratch_shapes=[pltpu.VMEM((tm + 2 * HALO, d), BF16)],
        compiler_params=_cparams(("arbitrary", "arbitrary", "arbitrary")),
        name="ffn_upact",
    )(x, x, x, gain, mod4, mod4, wup, wup, conv_w, conv_w, conv_b, conv_b)


def _rope_tables(n_tokens):
    rows = n_tokens // GRID_W
    row = jnp.broadcast_to(jnp.arange(rows, dtype=F32)[:, None], (rows, GRID_W)).reshape(-1)
    col = jnp.broadcast_to(jnp.arange(GRID_W, dtype=F32)[None, :], (rows, GRID_W)).reshape(-1)
    n_freq = DIFF_QK_DIM // 4
    inv = ROPE_BASE ** (-jnp.arange(n_freq, dtype=F32) / n_freq)
    ang = jnp.concatenate([row[:, None] * inv, col[:, None] * inv], axis=-1)
    cos, sin = jnp.cos(ang), jnp.sin(ang)
    cos_t = jnp.tile(cos, (1, LANES // cos.shape[1]))
    sin_t = jnp.tile(jnp.concatenate([-sin, sin], axis=-1), (1, LANES // (2 * sin.shape[1])))
    return cos_t, sin_t


def kernel(x, c, ctx, c_ctx, w_ada, b_ada, g_norm1, w_in, b_gate, conv_a_w, conv_a_b, ln_a_g, ln_a_b, w_a_out,
           qn_g, kn_g, lam_q1, lam_k1, lam_q2, lam_k2, subln_g, w_b_out, w_alpha2, b_alpha, gn_c_g, w_c_out,
           w_o, g_norm2, w_up, conv_f_w, conv_f_b, w_down):
    b_, n_lat, d = x.shape
    n_ctx = ctx.shape[1]
    depth = w_ada.shape[0]
    da = conv_a_w.shape[-1]
    qk_w = 2 * N_DIFF_HEADS * DIFF_QK_DIM
    db = w_b_out.shape[1]
    dc = w_c_out.shape[1]
    dk = w_alpha2.shape[-1]
    dff = w_down.shape[1]

    sizes = (2 * da, qk_w, qk_w, db, dk, dk, dc, dc, 2 * GATE_RANK, 3 * d)
    offs = np.concatenate([[0], np.cumsum(sizes)])
    o_a, o_qd, o_kd, o_vd, o_qg, o_kg, o_vg, o_r, o_lr, o_gate = (int(v) for v in offs[:-1])
    new_order = [(o_gate, 3 * d), (o_a, 2 * da), (o_qd, qk_w), (o_kd, qk_w), (o_vd, db), (o_vg, dc), (o_r, dc),
                 (o_qg, dk), (o_kg, dk), (o_lr, 2 * GATE_RANK)]
    starts = np.concatenate([[0], np.cumsum([s for _, s in new_order])])
    p_gate, p_a, p_qd, p_kd, p_vd, p_vg, p_r, p_qg, p_kg, p_lr = (int(v) for v in starts[:-1])
    tn_in = 768
    np_cols = -(-(p_lr + LANES) // tn_in) * tn_in
    assert p_a % (2 * da) == 0 and p_qd % qk_w == 0 and p_kd % qk_w == 0 and p_vd % LANES == 0
    assert p_vg % dc == 0 and p_r % dc == 0 and p_qg % dk == 0 and p_kg % dk == 0 and p_lr % LANES == 0

    def relayout_w_in(w):
        parts = [w[:, o:o + s] for o, s in new_order]
        parts.append(jnp.zeros((w.shape[0], np_cols - (p_lr + 2 * GATE_RANK)), w.dtype))
        return jnp.concatenate(parts, axis=1).astype(BF16)

    cos_l, sin_l = _rope_tables(n_lat)
    cos_c = jnp.ones((n_ctx, LANES), F32)
    sin_c = jnp.zeros((n_ctx, LANES), F32)
    grp = np.arange(LANES) // DIFF_QK_DIM
    gmat = jnp.asarray((grp[:, None] == grp[None, :]).astype(np.float32)).astype(BF16)

    n_cond = 16
    cvecs = jnp.zeros((n_cond, d), F32).at[:b_].set(c).at[b_].set(c_ctx)
    ctx_row = b_

    s_zero = jnp.zeros((b_, N_GLA_HEADS, dc // N_GLA_HEADS, dk // N_GLA_HEADS), F32)
    wa, wb, wc = w_a_out.astype(BF16), w_b_out.astype(BF16), w_c_out.astype(BF16)
    wo, wup, wdn = w_o.astype(BF16), w_up.astype(BF16), w_down.astype(BF16)

    for l in range(depth):
        last = l == depth - 1
        lam_init = 0.8 - 0.6 * math.exp(-0.3 * l)
        lam = (jnp.exp(jnp.sum(lam_q1[l] * lam_k1[l])) - jnp.exp(jnp.sum(lam_q2[l] * lam_k2[l])) + lam_init)
        lam_v = jnp.full((1, LANES), lam, F32)
        og = (subln_g[l] * (1.0 - lam_init)).reshape(1, LANES)
        gq = (jnp.tile(qn_g[l], LANES // DIFF_QK_DIM) * (DIFF_QK_DIM ** -0.5 * math.log2(math.e))).reshape(1, LANES)
        gk = jnp.tile(kn_g[l], LANES // DIFF_QK_DIM).reshape(1, LANES)

        mod = _adaln(cvecs, w_ada, b_ada, l)
        mod4 = mod.reshape(n_cond, 6, 1, d)

        w_in_p = relayout_w_in(w_in[l])
        g1 =g_norm1[l].reshape(1, d)
        g2 = g_norm2[l].reshape(1, d)
        bg = b_gate[l].reshape(3, 1, d)
        w2 = w_alpha2[l]
        w2p = [jnp.zeros((LANES, dk), F32).at[z * GATE_RANK:(z + 1) * GATE_RANK].set(w2[z]) for z in range(2)]
        bal = [b_alpha[l, z].reshape(1, dk) for z in range(2)]
        gla_cbs = (p_qg // dk, p_kg // dk, p_vg // dc, p_lr // LANES, dk, dc)

        u_l = _mm(x, w_in_p, tn=2432, out_dtype=BF16, norm=(g1, mod4, 0, 1, 0, 1), name="in_proj")
        ctx_flat = ctx.reshape(1, b_ * n_ctx, d)
        u_c = _mm(ctx_flat, w_in_p, tn=768, tm=1024, out_dtype=BF16, norm=(g1, mod4, 0, 1, ctx_row, 0),
                  name="in_proj_ctx").reshape(b_, n_ctx, np_cols)

        qp_l, kp_l = _qkprep(u_l, cos_l, sin_l, gq, gk, gmat, p_qd // qk_w, p_kd // qk_w)
        qp_c, kp_c = _qkprep(u_c, cos_c, sin_c, gq, gk, gmat, p_qd // qk_w, p_kd // qk_w)
        hb_l = _attn(qp_l, kp_l, u_l, p_vd // LANES, lam_v, og, ctx=(kp_c, u_c))

        of_c, sf = _gla_dir(u_c, gla_cbs, w2p[0], bal[0], s_zero, False)
        ob_c, sb = _gla_dir(u_c, gla_cbs, w2p[1], bal[1], s_zero, True)
        of_l, _ = _gla_dir(u_l, gla_cbs, w2p[0], bal[0], sf, False)
        ob_l, _ = _gla_dir(u_l, gla_cbs, w2p[1], bal[1], sb, True)
        gn = gn_c_g[l].reshape(1, -1)

        cb_a, lg_a, lb_a = conv_a_b[l].reshape(1, da), ln_a_g[l].reshape(1, da), ln_a_b[l].reshape(1, da)
        ha_l = _conformer(u_l, p_a // (2 * da), conv_a_w[l], cb_a, lg_a, lb_a)

        m_l = _merge(ha_l, hb_l, of_l, ob_l, u_l, p_r // dc, gn, wa, wb, wc, bg, l)
        x_mid = _mm(m_l, wo, tn=2048, out_dtype=F32, residual=(x, mod4, 2, 0, 1), layer=l, name="out_proj")

        cfb = conv_f_b[l].reshape(1, 2 * dff)
        act_l = _ffn_upact(x_mid, wup, l, conv_f_w[l], cfb, g2, mod4, 3, 4, 0, 1)
        x_new = _mm(act_l, wdn, tn=512, tm=1024, out_dtype=F32, residual=(x_mid, mod4, 5, 0, 1), layer=l,
                    name="ffn_down")

        if not last:
            hb_c = _attn(qp_c, kp_c, u_c, p_vd // LANES, lam_v, og)
            ha_c = _conformer(u_c, p_a // (2 * da), conv_a_w[l], cb_a, lg_a, lb_a)
            m_c = _merge(ha_c, hb_c, of_c, ob_c, u_c, p_r // dc, gn, wa, wb, wc, bg, l)
            ctx_mid = _mm(m_c.reshape(1, b_ * n_ctx, d), wo, tn=1024, tm=1024, out_dtype=F32,
                          residual=(ctx_flat, mod4, 2, ctx_row, 0), layer=l, name="out_proj_ctx")
            act_c = _ffn_upact(ctx_mid.reshape(b_, n_ctx, d), wup, l, conv_f_w[l], cfb, g2, mod4, 3, 4, ctx_row, 0)
            ctx = _mm(act_c.reshape(1, b_ * n_ctx, dff), wdn, tn=512, tm=1024, out_dtype=F32,
                      residual=(ctx_mid, mod4, 5, ctx_row, 0), layer=l, name="ffn_down_ctx").reshape(b_, n_ctx, d)

        x = x_new

    return x
```
